```python
import jax
import jax.numpy as jnp
from jax import lax
import numpy as np

D_MODEL = 2048
BATCH = 16
SEQ = 256
DEPTH = 4
DEC_BATCH = 8
DEC_SEQ = 2048
PAST_LEN = 512

GRID_W = 64
N_MIXERS = 3
MIXER_OF_LAYER = tuple(i % N_MIXERS for i in range(DEPTH))
LAYER_SLOT = tuple(MIXER_OF_LAYER[:i].count(MIXER_OF_LAYER[i]) for i in range(DEPTH))
N_POOL_LAYERS = MIXER_OF_LAYER.count(0)
N_RWKV_LAYERS = MIXER_OF_LAYER.count(1)
N_ATTN_LAYERS = MIXER_OF_LAYER.count(2)

POOL_WINDOWS = (2, 4, 8, 16)
POOL_GROUP = D_MODEL // len(POOL_WINDOWS)

RWKV_HEAD = 64
RWKV_HEADS = D_MODEL // RWKV_HEAD
DECAY_LORA = 96
AAA_LORA = 96
GATE_LORA = 256
DECAY_SCALE = 0.606531
GN_EPS = 64e-5

ATTN_HEAD_DIM = 128
ATTN_HEADS = 16
ATTN_KV_HEADS = 4
ATTN_GROUP = ATTN_HEADS // ATTN_KV_HEADS
WINDOW = 128
BLOCK = 128
ROPE_BASE = 10000.0
ATTN_SCALE = ATTN_HEAD_DIM ** -0.5
NEG_INF = -1e30

D_FF = 5632
CONV_WIDTH = 3
NORM_EPS = 1e-6

kernel_name = 'hybrid_pool_rwkv7_swa_dit_step'


def rmsnorm(x, g):
    x32 = x.astype(jnp.float32)
    y = x32 * lax.rsqrt(jnp.mean(x32 * x32, axis=-1, keepdims=True) + NORM_EPS)
    return (y * g.astype(jnp.float32)).astype(x.dtype)


def adaln(cond, w, b):
    m = jax.nn.silu(cond) @ w + b
    return jnp.split(m[:, None, :], 6, axis=-1)


def modulate(x, shift, scale):
    return x * (1.0 + scale) + shift


def centred_shift(x):
    xp = jnp.pad(x, ((0, 0), (1, 1), (0, 0)))
    return 0.5 * (xp[:, :-2] + xp[:, 2:]) - x


def pool_mixer(x, w_grp, scale):
    B, T, D = x.shape
    x32 = x.astype(jnp.float32)
    cs = jnp.concatenate([jnp.zeros((B, 1, D), jnp.float32), jnp.cumsum(x32, axis=1)], axis=1)
    t = jnp.arange(T)
    outs = []
    for gi, win in enumerate(POOL_WINDOWS):
        left = win // 2
        right = win - 1 - left
        lo = jnp.clip(t - left, 0, T)
        hi = jnp.clip(t + right + 1, 0, T)
        csg = cs[:, :, gi * POOL_GROUP:(gi + 1) * POOL_GROUP]
        s = jnp.take(csg, hi, axis=1) - jnp.take(csg, lo, axis=1)
        cnt = (hi - lo).astype(jnp.float32)[None, :, None]
        outs.append(s / cnt - x32[:, :, gi * POOL_GROUP:(gi + 1) * POOL_GROUP])
    pooled = jnp.stack(outs, axis=2).astype(x.dtype)
    y = jnp.einsum('btgi,gio->btgo', pooled, w_grp).reshape(B, T, D)
    return y * scale


def rwkv_scan(s0, r, w, k, v, aa, bb, reverse):
    def step(s, inp):
        r_t, w_t, k_t, v_t, a_t, b_t = inp
        sa = jnp.einsum('bhvk,bhk->bhv', s, a_t)
        s = s * w_t[:, :, None, :] + sa[..., None] * b_t[:, :, None, :] + v_t[..., None] * k_t[:, :, None, :]
        return s, jnp.einsum('bhvk,bhk->bhv', s, r_t)
    xs = tuple(jnp.moveaxis(z, 1, 0) for z in (r, w, k, v, aa, bb))
    s_final, o = lax.scan(step, s0.astype(jnp.float32), xs, reverse=reverse)
    return s_final, jnp.moveaxis(o, 0, 1)


def rwkv_mixer(h, p, s0_fwd, s0_bwd):
    B, T, D = h.shape
    H, N = RWKV_HEADS, RWKV_HEAD
    f32 = jnp.float32
    xx = centred_shift(h)
    mu = p['mu']
    xr, xw, xk, xv, xa, xg = (h + xx * mu[i] for i in range(6))
    r = (xr @ p['w_r']).reshape(B, T, H, N).astype(f32)
    k = (xk @ p['w_k']).reshape(B, T, H, N).astype(f32)
    v = (xv @ p['w_v']).reshape(B, T, H, N).astype(f32)
    g = jax.nn.sigmoid(xg @ p['g1']) @ p['g2']
    w_lora = jnp.einsum('ebtr,erd->ebtd', jnp.tanh(jnp.einsum('btd,edr->ebtr', xw, p['w1'])), p['w2'])
    decay = jnp.exp(-DECAY_SCALE * jax.nn.sigmoid((p['w0'][:, None, None, :] + w_lora).astype(f32)))
    decay = decay.reshape(2, B, T, H, N)
    a_lora = jnp.einsum('ebtr,erd->ebtd', jnp.einsum('btd,edr->ebtr', xa, p['a1']), p['a2'])
    a = jax.nn.sigmoid((p['a0'][:, None, None, :] + a_lora).astype(f32)).reshape(2, B, T, H, N)
    kk = k * p['k_k'].reshape(H, N).astype(f32)
    kk = kk / jnp.maximum(jnp.sqrt(jnp.sum(kk * kk, axis=-1, keepdims=True)), 1e-12)
    k_dir = k[None] * (1.0 + (a - 1.0) * p['k_a'].reshape(H, N).astype(f32))
    s_fwd, o_fwd = rwkv_scan(s0_fwd, r, decay[0], k_dir[0], v, -kk, kk * a[0], reverse=False)
    s_bwd, o_bwd = rwkv_scan(s0_bwd, r, decay[1], k_dir[1], v, -kk, kk * a[1], reverse=True)
    o = o_fwd + o_bwd
    mean = jnp.mean(o, axis=-1, keepdims=True)
    var = jnp.mean(jnp.square(o - mean), axis=-1, keepdims=True)
    o = ((o - mean) * lax.rsqrt(var + GN_EPS)).reshape(B, T, D) * p['ln_w'].astype(f32) + p['ln_b'].astype(f32)
    bonus = jnp.sum(r[None] * k_dir * p['r_k'].astype(f32), axis=(0, -1))[..., None] * v
    y = ((o + bonus.reshape(B, T, D)) * g.astype(f32)).astype(h.dtype)
    return y @ p['w_o'], s_fwd, s_bwd


def attn_qkv(h, p):
    B, T, _ = h.shape
    H, KV, Dh = ATTN_HEADS, ATTN_KV_HEADS, ATTN_HEAD_DIM
    qkv = h @ p['w_qkv']
    q = qkv[..., :H * Dh].reshape(B, T, H, Dh)
    k = qkv[..., H * Dh:(H + KV) * Dh].reshape(B, T, KV, Dh)
    v = qkv[..., (H + KV) * Dh:].reshape(B, T, KV, Dh)
    return rmsnorm(q, p['q_norm']), rmsnorm(k, p['k_norm']), v


def axial_rope_tables(T):
    rows = T // GRID_W
    row = jnp.broadcast_to(jnp.arange(rows)[:, None], (rows, GRID_W)).reshape(T).astype(jnp.float32)
    col = jnp.broadcast_to(jnp.arange(GRID_W)[None, :], (rows, GRID_W)).reshape(T).astype(jnp.float32)
    n_freq = ATTN_HEAD_DIM // 4
    inv = ROPE_BASE ** (-jnp.arange(n_freq, dtype=jnp.float32) / n_freq)
    ang = jnp.stack([row[:, None] * inv, col[:, None] * inv], axis=1)
    return jnp.cos(ang), jnp.sin(ang)


def apply_axial_rope(x, cos, sin):
    B, T, Hh, Dh = x.shape
    xs = x.reshape(B, T, Hh, 2, 2, Dh // 4).astype(jnp.float32)
    x1, x2 = xs[..., 0, :], xs[..., 1, :]
    c = cos[None, :, None]
    s = sin[None, :, None]
    out = jnp.stack([x1 * c - x2 * s, x2 * c + x1 * s], axis=-2)
    return out.reshape(B, T, Hh, Dh).astype(x.dtype)


def sink_softmax_attend(q_blk, keys, vals, sink, mask):
    B, Q = q_blk.shape[:2]
    scores = [jnp.einsum('bqkgd,bskd->bkgqs', q_blk, kk).astype(jnp.float32) * ATTN_SCALE for kk in keys]
    if mask is not None:
        scores[0] = jnp.where(mask[None, None, None], scores[0], NEG_INF)
    sink_col = jnp.broadcast_to(sink.reshape(ATTN_KV_HEADS, ATTN_GROUP)[None, :, :, None, None].astype(jnp.float32),
                                (B, ATTN_KV_HEADS, ATTN_GROUP, Q, 1))
    prob = jax.nn.softmax(jnp.concatenate(scores + [sink_col], axis=-1), axis=-1)
    out = None
    off = 0
    for vv in vals:
        n = vv.shape[1]
        o = jnp.einsum('bkgqs,bskd->bqkgd', prob[..., off:off + n].astype(vv.dtype), vv)
        out = o if out is None else out + o
        off += n
    return out


def attn_context(h, p):
    B, T, _ = h.shape
    q, k, v = attn_qkv(h, p)
    nq = T // BLOCK
    qb = jnp.moveaxis(q.reshape(B, nq, BLOCK, ATTN_KV_HEADS, ATTN_GROUP, ATTN_HEAD_DIM), 1, 0)
    o = lax.map(lambda qblk: sink_softmax_attend(qblk, [k], [v], p['sink'], None), qb)
    o = jnp.moveaxis(o, 0, 1).reshape(B, T, ATTN_HEADS * ATTN_HEAD_DIM)
    return o @ p['w_o'], k, v


def attn_latent(h, p, k_ctx, v_ctx):
    B, T, _ = h.shape
    q, k, v = attn_qkv(h, p)
    cos, sin = axial_rope_tables(T)
    q = apply_axial_rope(q, cos, sin)
    k = apply_axial_rope(k, cos, sin)
    nb = T // BLOCK
    qb = jnp.moveaxis(q.reshape(B, nb, BLOCK, ATTN_KV_HEADS, ATTN_GROUP, ATTN_HEAD_DIM), 1, 0)

    def band(z):
        zp = jnp.pad(z, ((0, 0), (BLOCK, BLOCK), (0, 0), (0, 0))).reshape(B, nb + 2, BLOCK, ATTN_KV_HEADS, ATTN_HEAD_DIM)
        zb = jnp.concatenate([zp[:, :-2], zp[:, 1:-1], zp[:, 2:]], axis=2)
        return jnp.moveaxis(zb, 1, 0)

    kb, vb = band(k), band(v)
    blk = jnp.arange(nb)[:, None, None] * BLOCK
    qpos = blk + jnp.arange(BLOCK)[None, :, None]
    kpos = blk + jnp.arange(3 * BLOCK)[None, None, :] - BLOCK
    valid = (jnp.abs(qpos - kpos) <= WINDOW) & (kpos >= 0) & (kpos < T)
    o = lax.map(lambda a: sink_softmax_attend(a[0], [a[1], k_ctx], [a[2], v_ctx], p['sink'], a[3]),
                (qb, kb, vb, valid))
    o = jnp.moveaxis(o, 0, 1).reshape(B, T, ATTN_HEADS * ATTN_HEAD_DIM)
    return o @ p['w_o']


def conv_ffn(h, up, conv_w, conv_b, down):
    T = h.shape[1]
    u = h @ up
    pad = CONV_WIDTH // 2
    up_ = jnp.pad(u, ((0, 0), (pad, pad), (0, 0)))
    u = sum(up_[:, j:j + T] * conv_w[j] for j in range(CONV_WIDTH)) + conv_b
    gate, val = jnp.split(u, 2, axis=-1)
    return (jax.nn.silu(gate) * val) @ down


def setup_inputs(seed: int = 0) -> dict:
    key = jax.random.key(seed)
    ks = iter(jax.random.split(key, 64))

    def nrm(shape, scale):
        return jax.random.normal(next(ks), shape, jnp.float32) * scale

    D, F = D_MODEL, D_FF
    H, N = RWKV_HEADS, RWKV_HEAD
    KV, Dh = ATTN_KV_HEADS, ATTN_HEAD_DIM
    NP, NR, NA = N_POOL_LAYERS, N_RWKV_LAYERS, N_ATTN_LAYERS
    qkv_w = (ATTN_HEADS + 2 * KV) * Dh
    inp = {}
    inp['x_prompt'] = nrm((BATCH, SEQ, D), 1.0)
    inp['x_sample'] = nrm((DEC_BATCH, DEC_SEQ, D), 1.0)
    inp['state_rwkv'] = nrm((DEC_BATCH, NR, 2, H, N, N), 0.5)
    inp['cache_k'] = nrm((DEC_BATCH, NA, PAST_LEN, KV, Dh), 1.0)
    inp['cache_v'] = nrm((DEC_BATCH, NA, PAST_LEN, KV, Dh), 1.0)
    inp['c'] = nrm((DEC_BATCH, D), 1.0)
    inp['c_ctx'] = nrm((D,), 1.0)
    inp['ada_w'] = nrm((DEPTH, D, 6 * D), D ** -0.5)
    inp['ada_b'] = nrm((DEPTH, 6 * D), 0.02)
    inp['norm_mix'] = 1.0 + nrm((DEPTH, D), 0.02)
    inp['norm_ffn'] = 1.0 + nrm((DEPTH, D), 0.02)
    inp['ffn_up'] = nrm((DEPTH, D, 2 * F), D ** -0.5)
    inp['ffn_conv_w'] = nrm((DEPTH, CONV_WIDTH, 2 * F), CONV_WIDTH ** -0.5)
    inp['ffn_conv_b'] = nrm((DEPTH, 2 * F), 0.02)
    inp['ffn_down'] = nrm((DEPTH, F, D), F ** -0.5)
    inp['pool_w'] = nrm((NP, len(POOL_WINDOWS), POOL_GROUP, POOL_GROUP), POOL_GROUP ** -0.5)
    inp['pool_scale'] = 1.0 + nrm((NP, D), 0.1)
    inp['rwkv_mu'] = jax.random.uniform(next(ks), (NR, 6, D), jnp.float32)
    inp['rwkv_w_r'] = nrm((NR, D, D), D ** -0.5)
    inp['rwkv_w_k'] = nrm((NR, D, D), D ** -0.5)
    inp['rwkv_w_v'] = nrm((NR, D, D), D ** -0.5)
    inp['rwkv_w0'] = nrm((NR, 2, D), 0.5)
    inp['rwkv_w1'] = nrm((NR, 2, D, DECAY_LORA), D ** -0.5)
    inp['rwkv_w2'] = nrm((NR, 2, DECAY_LORA, D), 0.5 * DECAY_LORA ** -0.5)
    inp['rwkv_a0'] = nrm((NR, 2, D), 0.5)
    inp['rwkv_a1'] = nrm((NR, 2, D, AAA_LORA), D ** -0.5)
    inp['rwkv_a2'] = nrm((NR, 2, AAA_LORA, D), 0.5 * AAA_LORA ** -0.5)
    inp['rwkv_g1'] = nrm((NR, D, GATE_LORA), D ** -0.5)
    inp['rwkv_g2'] = nrm((NR, GATE_LORA, D), GATE_LORA ** -0.5)
    inp['rwkv_k_k'] = 0.85 + nrm((NR, D), 0.02)
    inp['rwkv_k_a'] = 1.0 + nrm((NR, D), 0.02)
    inp['rwkv_r_k'] = nrm((NR, H, N), 0.1)
    inp['rwkv_ln_w'] = 1.0 + nrm((NR, D), 0.02)
    inp['rwkv_ln_b'] = nrm((NR, D), 0.02)
    inp['rwkv_w_o'] = nrm((NR, D, D), D ** -0.5)
    inp['attn_w_qkv'] = nrm((NA, D, qkv_w), D ** -0.5)
    inp['attn_q_norm'] = 1.0 + nrm((NA, Dh), 0.02)
    inp['attn_k_norm'] = 1.0 + nrm((NA, Dh), 0.02)
    inp['attn_sink'] = nrm((NA, ATTN_HEADS), 0.5)
    inp['attn_w_o'] = nrm((NA, ATTN_HEADS * Dh, D), (ATTN_HEADS * Dh) ** -0.5)
    return inp


def reference(x_prompt, x_sample, state_rwkv, cache_k, cache_v, c, c_ctx,
              ada_w, ada_b, norm_mix, norm_ffn, ffn_up, ffn_conv_w, ffn_conv_b, ffn_down,
              pool_w, pool_scale,
              rwkv_mu, rwkv_w_r, rwkv_w_k, rwkv_w_v, rwkv_w0, rwkv_w1, rwkv_w2,
              rwkv_a0, rwkv_a1, rwkv_a2, rwkv_g1, rwkv_g2, rwkv_k_k, rwkv_k_a, rwkv_r_k,
              rwkv_ln_w, rwkv_ln_b, rwkv_w_o,
              attn_w_qkv, attn_q_norm, attn_k_norm, attn_sink, attn_w_o):
    xp, xs = x_prompt, x_sample
    rwkv_states, ctx_keys, ctx_vals = [], [], []
    for l in range(DEPTH):
        kind, slot = MIXER_OF_LAYER[l], LAYER_SLOT[l]
        sh1p, sc1p, g1p, sh2p, sc2p, g2p = adaln(c_ctx[None, :], ada_w[l], ada_b[l])
        sh1s, sc1s, g1s, sh2s, sc2s, g2s = adaln(c, ada_w[l], ada_b[l])
        hp = modulate(rmsnorm(xp, norm_mix[l]), sh1p, sc1p)
        hs = modulate(rmsnorm(xs, norm_mix[l]), sh1s, sc1s)
        if kind == 0:
            yp = pool_mixer(hp, pool_w[slot], pool_scale[slot])
            ys = pool_mixer(hs, pool_w[slot], pool_scale[slot])
        elif kind == 1:
            p = {'mu': rwkv_mu[slot], 'w_r': rwkv_w_r[slot], 'w_k': rwkv_w_k[slot], 'w_v': rwkv_w_v[slot],
                 'w0': rwkv_w0[slot], 'w1': rwkv_w1[slot], 'w2': rwkv_w2[slot],
                 'a0': rwkv_a0[slot], 'a1': rwkv_a1[slot], 'a2': rwkv_a2[slot],
                 'g1': rwkv_g1[slot], 'g2': rwkv_g2[slot], 'k_k': rwkv_k_k[slot], 'k_a': rwkv_k_a[slot],
                 'r_k': rwkv_r_k[slot], 'ln_w': rwkv_ln_w[slot], 'ln_b': rwkv_ln_b[slot], 'w_o': rwkv_w_o[slot]}
            zero = jnp.zeros((xp.shape[0], RWKV_HEADS, RWKV_HEAD, RWKV_HEAD), jnp.float32)
            yp, s_f, s_b = rwkv_mixer(hp, p, zero, zero)
            rwkv_states.append(jnp.stack([s_f, s_b], axis=1))
            ys, _, _ = rwkv_mixer(hs, p, state_rwkv[:, slot, 0], state_rwkv[:, slot, 1])
        else:
            p = {'w_qkv': attn_w_qkv[slot], 'q_norm': attn_q_norm[slot], 'k_norm': attn_k_norm[slot],
                 'sink': attn_sink[slot], 'w_o': attn_w_o[slot]}
            yp, k_c, v_c = attn_context(hp, p)
            ctx_keys.append(k_c)
            ctx_vals.append(v_c)
            ys = attn_latent(hs, p, cache_k[:, slot], cache_v[:, slot])
        xp = xp + g1p * yp
        xs = xs + g1s * ys
        hp = modulate(rmsnorm(xp, norm_ffn[l]), sh2p, sc2p)
        hs = modulate(rmsnorm(xs, norm_ffn[l]), sh2s, sc2s)
        xp = xp + g2p * conv_ffn(hp, ffn_up[l], ffn_conv_w[l], ffn_conv_b[l], ffn_down[l])
        xs = xs + g2s * conv_ffn(hs, ffn_up[l], ffn_conv_w[l], ffn_conv_b[l], ffn_down[l])
    y_prompt = xp
    y_sample = xs
    new_state_rwkv = jnp.stack(rwkv_states, axis=1)
    new_cache_k = jnp.stack(ctx_keys, axis=1)
    new_cache_v = jnp.stack(ctx_vals, axis=1)
    return (y_prompt, y_sample, new_state_rwkv, new_cache_k, new_cache_v)
```

```python
import functools
from typing import NamedTuple

import jax
import jax.numpy as jnp
from jax import lax
from jax.experimental import pallas as pl
from jax.experimental.pallas import tpu as pltpu

F32 = jnp.float32
BF16 = jnp.bfloat16

V7X_LANES = 128
V7X_SUBLANES = 8
V7X_VMEM_LIMIT_BYTES = 56 * 1024 * 1024

NORM_EPS = 1e-6
GN_EPS = 64e-5
DECAY_SCALE = 0.606531
POOL_WINDOWS = (2, 4, 8, 16)
RWKV_HEAD = 64
ATTN_HEAD_DIM = 128
ATTN_GROUP = 4
ATTN_BLOCK = 128
ATTN_WINDOW = 128
GRID_W = 64
ROPE_BASE = 10000.0
ATTN_SCALE = ATTN_HEAD_DIM ** -0.5
NEG_INF = -1e30
LORA_PAD = 128
SCAN_TB = 32


class _Cfg(NamedTuple):
    n_ctx: int
    l_ctx: int
    n_lat: int
    l_lat: int
    d: int
    m_ctx: int
    m: int
    tm: int


def _cparams(n_axes):
    return pltpu.CompilerParams(dimension_semantics=("arbitrary",) * n_axes,
                                vmem_limit_bytes=V7X_VMEM_LIMIT_BYTES)


def _cond_fn(cfg, tm):
    def f(i):
        r = i * tm
        return jnp.where(r < cfg.m_ctx, 0, 1 + (r - cfg.m_ctx) // cfg.l_lat)
    return f


def _seq_len_of_tile(cfg, i, tm):
    return jnp.where(i * tm < cfg.m_ctx, cfg.l_ctx, cfg.l_lat)


def _shift_rows(x, pos, seq_len, s):
    n = x.shape[0]
    if s > 0:
        return jnp.where(pos >= s, pltpu.roll(x, s, axis=0), 0.0)
    return jnp.where(pos < seq_len + s, pltpu.roll(x, n + s, axis=0), 0.0)


def _adaln_kernel(c_ref, w_ref, b_ref, o_ref):
    c = c_ref[...]
    s = c * jax.nn.sigmoid(c)
    acc = jnp.dot(s.astype(BF16), w_ref[...].astype(BF16), preferred_element_type=F32)
    o_ref[...] = acc + b_ref[...]


def _adaln(cond, ada_w, ada_b):
    depth, d, n6 = ada_w.shape
    ncp = cond.shape[0]
    tn = 1024
    return pl.pallas_call(
        _adaln_kernel,
        grid=(depth, n6 // tn),
        in_specs=[pl.BlockSpec((ncp, d), lambda l, j: (0, 0)),
                  pl.BlockSpec((None, d, tn), lambda l, j: (l, 0, j)),
                  pl.BlockSpec((None, 1, tn), lambda l, j: (l, 0, j))],
        out_specs=pl.BlockSpec((None, ncp, tn), lambda l, j: (l, 0, j)),
        out_shape=jax.ShapeDtypeStruct((depth, ncp, n6), F32),
        compiler_params=_cparams(2),
        name="adaln",
    )(cond, ada_w, ada_b.reshape(depth, 1, n6))


def _mod_spec1(cfg, mod_l, which, tm):
    cond = _cond_fn(cfg, tm)
    return pl.BlockSpec((None, None, None, 1, cfg.d), lambda i: (mod_l, which, cond(i), 0, 0))


def _mod_spec2(cfg, mod_l, which, tm, tn):
    cond = _cond_fn(cfg, tm)
    return pl.BlockSpec((None, None, None, 1, tn), lambda i, j: (mod_l, which, cond(i), 0, j))


def _norm_mod_kernel(x_ref, g_ref, sh_ref, sc_ref, o_ref):
    x = x_ref[...]
    y = x * lax.rsqrt(jnp.mean(x * x, axis=-1, keepdims=True) + NORM_EPS)
    y = y * g_ref[...]
    o_ref[...] = (y * (1.0 + sc_ref[...]) + sh_ref[...]).astype(o_ref.dtype)


def _norm_mod(cfg, x, g, mod, l, shift_i, scale_i, out_dtype):
    tm = min(512, cfg.tm)
    return pl.pallas_call(
        _norm_mod_kernel,
        grid=(cfg.m // tm,),
        in_specs=[pl.BlockSpec((tm, cfg.d), lambda i: (i, 0)),
                  pl.BlockSpec((None, 1, cfg.d), lambda i: (l, 0, 0)),
                  _mod_spec1(cfg, l, shift_i, tm),
                  _mod_spec1(cfg, l, scale_i, tm)],
        out_specs=pl.BlockSpec((tm, cfg.d), lambda i: (i, 0)),
        out_shape=jax.ShapeDtypeStruct((cfg.m, cfg.d), out_dtype),
        compiler_params=_cparams(1),
        name="norm_mod",
    )(x, g.reshape(g.shape[0], 1, cfg.d), mod, mod)


def _mm_kernel(n_pro, n_epi, pro, epi, x_ref, *refs):
    pro_refs = refs[:n_pro]
    w_ref = refs[n_pro]
    epi_refs = refs[n_pro + 1:n_pro + 1 + n_epi]
    o_ref = refs[n_pro + 1 + n_epi]
    x = x_ref[...]
    if pro is not None:
        x = pro(x, *pro_refs)
    acc = jnp.dot(x.astype(BF16), w_ref[...].astype(BF16), preferred_element_type=F32)
    if epi is not None:
        acc = epi(acc, *epi_refs)
    o_ref[...] = acc.astype(o_ref.dtype)


def _mm(x, w, w_spec, n, *, tm, tn, out_dtype, name, pro=None, pro_args=(), pro_specs=(),
        epi=None, epi_args=(), epi_specs=()):
    m, k = x.shape
    return pl.pallas_call(
        functools.partial(_mm_kernel, len(pro_args), len(epi_args), pro, epi),
        grid=(m // tm, n // tn),
        in_specs=[pl.BlockSpec((tm, k), lambda i, j: (i, 0)), *pro_specs, w_spec, *epi_specs],
        out_specs=pl.BlockSpec((tm, tn), lambda i, j: (i, j)),
        out_shape=jax.ShapeDtypeStruct((m, n), out_dtype),
        compiler_params=_cparams(2),
        name=name,
    )(x, *pro_args, w, *epi_args)


def _w2_spec(k, tn, joff=0):
    return pl.BlockSpec((k, tn), lambda i, j: (0, j + joff))


def _w3_spec(l, k, tn, joff=0):
    return pl.BlockSpec((None, k, tn), lambda i, j: (l, 0, j + joff))


def _row_spec(tn):
    return pl.BlockSpec((1, tn), lambda i, j: (0, j))


def _tile_spec(tm, tn):
    return pl.BlockSpec((tm, tn), lambda i, j: (i, j))


def _epi_resid(acc, x_ref, gate_ref):
    return x_ref[...] + gate_ref[...] * acc


def _epi_sigmoid(acc):
    return jax.nn.sigmoid(acc)


def _epi_tanh(acc):
    return jnp.tanh(acc)


def _epi_decay(acc, w0_ref):
    return jnp.exp(-DECAY_SCALE * jax.nn.sigmoid(w0_ref[...] + acc))


def _epi_bias_sigmoid(acc, a0_ref):
    return jax.nn.sigmoid(a0_ref[...] + acc)


def _pro_mul(x, g_ref):
    return x * g_ref[...]


def _pool_kernel(cfg, h_ref, x_ref, w_ref, sc_ref, gate_ref, o_ref):
    i = pl.program_id(0)
    g = pl.program_id(1)
    tm = h_ref.shape[0]
    seq_len = _seq_len_of_tile(cfg, i, tm)
    pos = lax.broadcasted_iota(jnp.int32, (tm, 1), 0) & (seq_len - 1)

    for gi, win in enumerate(POOL_WINDOWS):
        @pl.when(g == gi)
        def _(gi=gi, win=win):
            h = h_ref[...]
            left = win // 2
            right = win - 1 - left
            back = _shift_rows(h, pos, seq_len, 1)
            fwd = h
            n = 1
            while n < left:
                back = back + _shift_rows(back, pos, seq_len, n)
                fwd = fwd + _shift_rows(fwd, pos, seq_len, -n)
                n *= 2
            cnt = (jnp.minimum(pos + right + 1, seq_len) - jnp.maximum(pos - left, 0)).astype(F32)
            pooled = (back + fwd) / cnt - h
            y = jnp.dot(pooled.astype(BF16), w_ref[...].astype(BF16), preferred_element_type=F32)
            o_ref[...] = x_ref[...] + gate_ref[...] * (y * sc_ref[...])


def _pool_mixer(cfg, h, x, w_grp, scale, mod, l):
    ngrp, gsz, _ = w_grp.shape
    tm = cfg.tm
    return pl.pallas_call(
        functools.partial(_pool_kernel, cfg),
        grid=(cfg.m // tm, ngrp),
        in_specs=[_tile_spec(tm, gsz), _tile_spec(tm, gsz),
                  pl.BlockSpec((None, gsz, gsz), lambda i, j: (j, 0, 0)),
                  _row_spec(gsz),
                  _mod_spec2(cfg, l, 2, tm, gsz)],
        out_specs=_tile_spec(tm, gsz),
        out_shape=jax.ShapeDtypeStruct((cfg.m, cfg.d), F32),
        compiler_params=_cparams(2),
        name="pool_mixer",
    )(h, x, w_grp, scale.reshape(1, cfg.d), mod)


def _shift_mix_kernel(cfg, h_ref, mu_ref, *o_refs):
    i = pl.program_id(0)
    tm = h_ref.shape[0]
    seq_len = _seq_len_of_tile(cfg, i, tm)
    pos = lax.broadcasted_iota(jnp.int32, (tm, 1), 0) & (seq_len - 1)
    h = h_ref[...]
    xx = 0.5 * (_shift_rows(h, pos, seq_len, 1) + _shift_rows(h, pos, seq_len, -1)) - h
    for n, o_ref in enumerate(o_refs):
        o_ref[...] = (h + xx * mu_ref[n:n + 1, :]).astype(o_ref.dtype)


def _shift_mix(cfg, h, mu):
    tm, tn = cfg.tm, 512
    n_out = mu.shape[0]
    return pl.pallas_call(
        functools.partial(_shift_mix_kernel, cfg),
        grid=(cfg.m // tm, cfg.d // tn),
        in_specs=[_tile_spec(tm, tn), pl.BlockSpec((n_out, tn), lambda i, j: (0, j))],
        out_specs=[_tile_spec(tm, tn)] * n_out,
        out_shape=[jax.ShapeDtypeStruct((cfg.m, cfg.d), BF16)] * n_out,
        compiler_params=_cparams(2),
        name="rwkv_shift_mix",
    )(h, mu)


def _scan_kernel(tb_steps, r_ref, k_ref, v_ref, w_ref, ag_ref, kk_ref, ka_ref, rk_ref, s0_ref,
                 o_ref, bonus_ref, s_ref, a_s, b_s, kd_s):
    e = pl.program_id(0)
    tb = pl.program_id(2)
    n = RWKV_HEAD
    nvb = n // V7X_SUBLANES

    @pl.when(tb == 0)
    def _():
        s_ref[...] = s0_ref[...]

    k = k_ref[...]
    ag = ag_ref[...]
    kk = k * kk_ref[...][None]
    kk = kk / jnp.maximum(jnp.sqrt(jnp.sum(kk * kk, axis=1, keepdims=True)), 1e-12)
    a_s[...] = -kk
    b_s[...] = kk * ag
    kd = k * (1.0 + (ag - 1.0) * ka_ref[...][None])
    kd_s[...] = kd
    bonus_ref[...] = jnp.sum(r_ref[...] * kd * rk_ref[...][None], axis=1, keepdims=True)

    def vslice(vb):
        return pl.ds(vb * V7X_SUBLANES, V7X_SUBLANES)

    def step(s, carry):
        t = jnp.where(e == 0, s, tb_steps - 1 - s)

        def pass1(kb, acc):
            acc = list(acc)
            for kr in range(V7X_SUBLANES):
                ki = kb * V7X_SUBLANES + kr
                a_row = a_s[t, pl.ds(ki, 1), :]
                for vb in range(nvb):
                    acc[vb] = acc[vb] + s_ref[ki, vslice(vb), :] * a_row
            return tuple(acc)

        zeros = tuple(jnp.zeros((V7X_SUBLANES, V7X_LANES), F32) for _ in range(nvb))
        sa = lax.fori_loop(0, n // V7X_SUBLANES, pass1, zeros)
        vv = [v_ref[t, vslice(vb), :] for vb in range(nvb)]

        def pass2(kb, acc):
            acc = list(acc)
            for kr in range(V7X_SUBLANES):
                ki = kb * V7X_SUBLANES + kr
                w_row = w_ref[t, pl.ds(ki, 1), :]
                b_row = b_s[t, pl.ds(ki, 1), :]
                k_row = kd_s[t, pl.ds(ki, 1), :]
                r_row = r_ref[t, pl.ds(ki, 1), :]
                for vb in range(nvb):
                    sn = s_ref[ki, vslice(vb), :] * w_row + sa[vb] * b_row + vv[vb] * k_row
                    s_ref[ki, vslice(vb), :] = sn
                    acc[vb] = acc[vb] + sn * r_row
            return tuple(acc)

        o = lax.fori_loop(0, n // V7X_SUBLANES, pass2, zeros)
        for vb in range(nvb):
            o_ref[t, vslice(vb), :] = o[vb]
        return carry

    lax.fori_loop(0, tb_steps, step, 0)


def _rwkv_scan(r, k, v, w2, ag2, kk_p, ka_p, rk_p, s0):
    g, t, n, lanes = r.shape
    tb = min(SCAN_TB, t)
    nt = t // tb

    def tmap(e, tt):
        return jnp.where(e == 0, tt, nt - 1 - tt)

    shared = pl.BlockSpec((None, tb, n, lanes), lambda e, gg, tt: (gg, tmap(e, tt), 0, 0))
    per_dir = pl.BlockSpec((None, None, tb, n, lanes), lambda e, gg, tt: (e, gg, tmap(e, tt), 0, 0))
    par = pl.BlockSpec((n, lanes), lambda e, gg, tt: (0, 0))
    st = pl.BlockSpec((None, None, n, n, lanes), lambda e, gg, tt: (e, gg, 0, 0, 0))
    return pl.pallas_call(
        functools.partial(_scan_kernel, tb),
        grid=(2, g, nt),
        in_specs=[shared, shared, shared, per_dir, per_dir, par, par, par, st],
        out_specs=[per_dir,
                   pl.BlockSpec((None, None, tb, 1, lanes), lambda e, gg, tt: (e, gg, tmap(e, tt), 0, 0)),
                   st],
        out_shape=[jax.ShapeDtypeStruct((2, g, t, n, lanes), F32),
                   jax.ShapeDtypeStruct((2, g, t, 1, lanes), F32),
                   jax.ShapeDtypeStruct((2, g, n, n, lanes), F32)],
        scratch_shapes=[pltpu.VMEM((tb, n, lanes), F32)] * 3,
        compiler_params=_cparams(3),
        name="rwkv_scan",
    )(r, k, v, w2, ag2, kk_p, ka_p, rk_p, s0)


def _rwkv_post_kernel(of_ref, ob_ref, v_ref, bf_ref, bb_ref, lnw_ref, lnb_ref, y_ref):
    o = of_ref[...] + ob_ref[...]
    mean = jnp.mean(o, axis=1, keepdims=True)
    var = jnp.mean(jnp.square(o - mean), axis=1, keepdims=True)
    on = (o - mean) * lax.rsqrt(var + GN_EPS)
    y_ref[...] = on * lnw_ref[...][None] + lnb_ref[...][None] + (bf_ref[...] + bb_ref[...]) * v_ref[...]


def _rwkv_post(o2, v, bonus2, lnw_p, lnb_p):
    _, g, t, n, lanes = o2.shape
    tb = min(64, t)
    blk = lambda e: pl.BlockSpec((None, None, tb, n, lanes), lambda gg, tt: (e, gg, tt, 0, 0))
    bon = lambda e: pl.BlockSpec((None, None, tb, 1, lanes), lambda gg, tt: (e, gg, tt, 0, 0))
    par = pl.BlockSpec((n, lanes), lambda gg, tt: (0, 0))
    nat = pl.BlockSpec((None, tb, n, lanes), lambda gg, tt: (gg, tt, 0, 0))
    return pl.pallas_call(
        _rwkv_post_kernel,
        grid=(g, t // tb),
        in_specs=[blk(0), blk(1), nat, bon(0), bon(1), par, par],
        out_specs=nat,
        out_shape=jax.ShapeDtypeStruct((g, t, n, lanes), F32),
        compiler_params=_cparams(2),
        name="rwkv_post",
    )(o2, o2, v, bonus2, bonus2, lnw_p, lnb_p)


def _to_lanes(x, nseq, t):
    h = x.shape[1] // RWKV_HEAD
    per = V7X_LANES // h
    g = nseq // per
    return x.reshape(g, per, t, h, RWKV_HEAD).transpose(0, 2, 4, 1, 3).reshape(g, t, RWKV_HEAD, V7X_LANES)


def _from_lanes(y, nseq, t):
    g = y.shape[0]
    per = nseq // g
    h = V7X_LANES // per
    return y.reshape(g, t, RWKV_HEAD, per, h).transpose(0, 3, 1, 4, 2).reshape(nseq * t, h * RWKV_HEAD)


def _param_lanes(p, per):
    h = p.size // RWKV_HEAD
    return jnp.tile(p.reshape(h, RWKV_HEAD).T, (1, per))


def _pad_lora_in(w):
    r = w.shape[-1]
    w = jnp.pad(w, ((0, 0), (0, 0), (0, LORA_PAD - r)))
    return jnp.concatenate([w[0], w[1]], axis=1)


def _pad_lora_out(w):
    r = w.shape[1]
    z = jnp.zeros((LORA_PAD, w.shape[2]), w.dtype)
    wp = jnp.pad(w, ((0, 0), (0, LORA_PAD - r), (0, 0)))
    return jnp.stack([jnp.concatenate([wp[0], z], axis=0), jnp.concatenate([z, wp[1]], axis=0)])


def _rwkv_mixer(cfg, h, x, p, state, mod, l):
    d = cfg.d
    tm = min(1024, cfg.tm)
    xr, xw, xk, xv, xa, xg = _shift_mix(cfg, h, p['mu'])
    mmf = functools.partial(_mm, tm=tm, tn=512)
    r = mmf(xr, p['w_r'], _w2_spec(d, 512), d, out_dtype=F32, name="rwkv_r")
    k = mmf(xk, p['w_k'], _w2_spec(d, 512), d, out_dtype=F32, name="rwkv_k")
    v = mmf(xv, p['w_v'], _w2_spec(d, 512), d, out_dtype=F32, name="rwkv_v")
    gl = _mm(xg, p['g1'], _w2_spec(d, 256), 256, tm=tm, tn=256, out_dtype=BF16, name="rwkv_g1", epi=_epi_sigmoid)
    gate = mmf(gl, p['g2'], _w2_spec(256, 512), d, out_dtype=F32, name="rwkv_g2")
    nl = 2 * LORA_PAD
    wl = _mm(xw, _pad_lora_in(p['w1']), _w2_spec(d, nl), nl, tm=tm, tn=nl, out_dtype=BF16, name="rwkv_w1",
             epi=_epi_tanh)
    al = _mm(xa, _pad_lora_in(p['a1']), _w2_spec(d, nl), nl, tm=tm, tn=nl, out_dtype=BF16, name="rwkv_a1")
    w2p = _pad_lora_out(p['w2'])
    a2p = _pad_lora_out(p['a2'])
    dec, agt = [], []
    for e in range(2):
        dec.append(mmf(wl, w2p[e], _w2_spec(nl, 512), d, out_dtype=F32, name="rwkv_w2", epi=_epi_decay,
                       epi_args=(p['w0'][e].reshape(1, d),), epi_specs=(_row_spec(512),)))
        agt.append(mmf(al, a2p[e], _w2_spec(nl, 512), d, out_dtype=F32, name="rwkv_a2", epi=_epi_bias_sigmoid,
                       epi_args=(p['a0'][e].reshape(1, d),), epi_specs=(_row_spec(512),)))

    n_heads = d // RWKV_HEAD
    per = V7X_LANES // n_heads
    kk_p = _param_lanes(p['k_k'], per)
    ka_p = _param_lanes(p['k_a'], per)
    rk_p = _param_lanes(p['r_k'], per)
    lnw_p = _param_lanes(p['ln_w'], per)
    lnb_p = _param_lanes(p['ln_b'], per)

    ys, finals = [], []
    for seg, (nseq, t, r0) in enumerate(((cfg.n_ctx, cfg.l_ctx, 0), (cfg.n_lat, cfg.l_lat, cfg.m_ctx))):
        rows = slice(r0, r0 + nseq * t)
        tl = lambda z: _to_lanes(z[rows], nseq, t)
        g = nseq // per
        if seg == 0:
            s0 = jnp.zeros((2, g, RWKV_HEAD, RWKV_HEAD, V7X_LANES), F32)
        else:
            s0 = state.reshape(g, per, 2, n_heads, RWKV_HEAD, RWKV_HEAD).transpose(2, 0, 5, 4, 1, 3)
            s0 = s0.reshape(2, g, RWKV_HEAD, RWKV_HEAD, V7X_LANES)
        rc, kc, vc = tl(r), tl(k), tl(v)
        w2 = jnp.stack([tl(dec[0]), tl(dec[1])])
        ag2 = jnp.stack([tl(agt[0]), tl(agt[1])])
        o2, bonus2, sfin = _rwkv_scan(rc, kc, vc, w2, ag2, kk_p, ka_p, rk_p, s0)
        ys.append(_from_lanes(_rwkv_post(o2, vc, bonus2, lnw_p, lnb_p), nseq, t))
        finals.append(sfin)
    y = jnp.concatenate(ys, axis=0)

    sfin = finals[0]
    g = cfg.n_ctx // per
    new_state = sfin.reshape(2, g, RWKV_HEAD, RWKV_HEAD, per, n_heads).transpose(1, 4, 0, 5, 3, 2)
    new_state = new_state.reshape(cfg.n_ctx, 2, n_heads, RWKV_HEAD, RWKV_HEAD)

    tmo = min(512, tm)
    x_new = _mm(y, p['w_o'], _w2_spec(d, 512), d, tm=tmo, tn=512, out_dtype=F32, name="rwkv_wo",
                pro=_pro_mul, pro_args=(gate,), pro_specs=(pl.BlockSpec((tmo, d), lambda i, j: (i, 0)),),
                epi=_epi_resid, epi_args=(x, mod), epi_specs=(_tile_spec(tmo, 512), _mod_spec2(cfg, l, 2, tmo, 512)))
    return x_new, new_state


def _head_rmsnorm(acc, w_ref):
    outs = []
    for hh in range(acc.shape[1] // ATTN_HEAD_DIM):
        xh = acc[:, hh * ATTN_HEAD_DIM:(hh + 1) * ATTN_HEAD_DIM]
        yh = xh * lax.rsqrt(jnp.mean(xh * xh, axis=-1, keepdims=True) + NORM_EPS)
        outs.append(yh * w_ref[...])
    return jnp.concatenate(outs, axis=1) if len(outs) > 1 else outs[0]


def _rope(y, cos_ref, sin_ref):
    nh = y.shape[1] // ATTN_HEAD_DIM
    cos = jnp.concatenate([cos_ref[...]] * nh, axis=1) if nh > 1 else cos_ref[...]
    sin = jnp.concatenate([sin_ref[...]] * nh, axis=1) if nh > 1 else sin_ref[...]
    width = y.shape[1]
    lane = lax.broadcasted_iota(jnp.int32, (1, width), 1)
    quarter = ATTN_HEAD_DIM // 4
    first = (lane & (2 * quarter - 1)) < quarter
    swapped = jnp.where(first, pltpu.roll(y, width - quarter, axis=1), pltpu.roll(y, quarter, axis=1))
    return y * cos + swapped * sin


def _epi_qk(acc, nw_ref, cos_ref, sin_ref):
    return _rope(_head_rmsnorm(acc, nw_ref), cos_ref, sin_ref)


def _rope_tables(cfg, tm):
    t = cfg.l_lat
    row = (jnp.arange(t) // GRID_W).astype(F32)
    col = (jnp.arange(t) % GRID_W).astype(F32)
    n_freq = ATTN_HEAD_DIM // 4
    inv = ROPE_BASE ** (-jnp.arange(n_freq, dtype=F32) / n_freq)
    ar, ac = row[:, None] * inv, col[:, None] * inv
    cos = jnp.concatenate([jnp.cos(ar), jnp.cos(ar), jnp.cos(ac), jnp.cos(ac)], axis=1)
    sin = jnp.concatenate([-jnp.sin(ar), jnp.sin(ar), -jnp.sin(ac), jnp.sin(ac)], axis=1)
    cos = jnp.concatenate([jnp.ones((tm, ATTN_HEAD_DIM), F32), cos], axis=0)
    sin = jnp.concatenate([jnp.zeros((tm, ATTN_HEAD_DIM), F32), sin], axis=0)
    return cos, sin


def _attn_kernel(n_band, nb, q_ref, *refs):
    n_src = (len(refs) - 2) // 2
    k_refs = refs[:n_src]
    v_refs = refs[n_src:2 * n_src]
    sink_ref = refs[2 * n_src]
    o_ref = refs[2 * n_src + 1]
    blk = pl.program_id(2)
    bq = q_ref.shape[0]
    q = jnp.concatenate([q_ref[:, gq * ATTN_HEAD_DIM:(gq + 1) * ATTN_HEAD_DIM] for gq in range(ATTN_GROUP)],
                        axis=0).astype(BF16)
    scores = []
    for si in range(n_src):
        s = lax.dot_general(q, k_refs[si][...].astype(BF16), (((1,), (1,)), ((), ())),
                            preferred_element_type=F32) * ATTN_SCALE
        if si < n_band and n_band == 3 and si != 1:
            qi = lax.broadcasted_iota(jnp.int32, (ATTN_GROUP * bq, bq), 0) & (bq - 1)
            ki = lax.broadcasted_iota(jnp.int32, (ATTN_GROUP * bq, bq), 1)
            if si == 0:
                valid, present = ki >= qi, blk > 0
            else:
                valid, present = ki <= qi, blk < nb - 1
            s = jnp.where(valid, s, NEG_INF) + jnp.where(present, 0.0, NEG_INF)
        scores.append(s)
    sink = sink_ref[...]
    m = sink
    for s in scores:
        m = jnp.maximum(m, jnp.max(s, axis=-1, keepdims=True))
    ps = [jnp.exp(s - m) for s in scores]
    den = jnp.exp(sink - m)
    for p in ps:
        den = den + jnp.sum(p, axis=-1, keepdims=True)
    inv = 1.0 / den
    out = None
    for si in range(n_src):
        o = jnp.dot((ps[si] * inv).astype(BF16), v_refs[si][...].astype(BF16), preferred_element_type=F32)
        out = o if out is None else out + o
    for gq in range(ATTN_GROUP):
        o_ref[:, gq * ATTN_HEAD_DIM:(gq + 1) * ATTN_HEAD_DIM] = out[gq * bq:(gq + 1) * bq, :].astype(o_ref.dtype)


def _attn_mixer(cfg, h, x, p, cache_k, cache_v, mod, l):
    d, tm = cfg.d, min(1024, cfg.tm)
    dh = ATTN_HEAD_DIM
    nkv = cache_k.shape[-2]
    nq = nkv * ATTN_GROUP
    cos, sin = _rope_tables(cfg, tm)
    n_ctx_tiles = cfg.m_ctx // tm
    lat_tiles = cfg.l_lat // tm

    def rope_spec():
        return pl.BlockSpec((tm, dh), lambda i, j: (jnp.where(i < n_ctx_tiles, 0, 1 + (i - n_ctx_tiles) % lat_tiles), 0))

    norm_spec = pl.BlockSpec((1, dh), lambda i, j: (0, 0))
    tn = 512
    q = _mm(h, p['w_qkv'], _w2_spec(d, tn), nq * dh, tm=tm, tn=tn, out_dtype=BF16, name="attn_q", epi=_epi_qk,
            epi_args=(p['q_norm'].reshape(1, dh), cos, sin), epi_specs=(norm_spec, rope_spec(), rope_spec()))
    k = _mm(h, p['w_qkv'], _w2_spec(d, tn, nq * dh // tn), nkv * dh, tm=tm, tn=tn, out_dtype=F32, name="attn_k",
            epi=_epi_qk, epi_args=(p['k_norm'].reshape(1, dh), cos, sin),
            epi_specs=(norm_spec, rope_spec(), rope_spec()))
    v = _mm(h, p['w_qkv'], _w2_spec(d, tn, (nq + nkv) * dh // tn), nkv * dh, tm=tm, tn=tn, out_dtype=F32,
            name="attn_v")

    bq = ATTN_BLOCK
    sink_col = jnp.repeat(p['sink'].reshape(nkv, ATTN_GROUP), bq, axis=1).reshape(nkv, ATTN_GROUP * bq, 1)
    sink_spec = pl.BlockSpec((None, ATTN_GROUP * bq, 1), lambda b, kv, qb: (kv, 0, 0))

    nqb = cfg.l_ctx // bq
    o_ctx = pl.pallas_call(
        functools.partial(_attn_kernel, 0, nqb),
        grid=(cfg.n_ctx, nkv, nqb),
        in_specs=[pl.BlockSpec((bq, ATTN_GROUP * dh), lambda b, kv, qb: (b * nqb + qb, kv)),
                  pl.BlockSpec((cfg.l_ctx, dh), lambda b, kv, qb: (b, kv)),
                  pl.BlockSpec((cfg.l_ctx, dh), lambda b, kv, qb: (b, kv)),
                  sink_spec],
        out_specs=pl.BlockSpec((bq, ATTN_GROUP * dh), lambda b, kv, qb: (b * nqb + qb, kv)),
        out_shape=jax.ShapeDtypeStruct((cfg.m_ctx, d), BF16),
        compiler_params=_cparams(3),
        name="attn_context",
    )(q, k, v, sink_col)

    nb = cfg.l_lat // bq
    off = cfg.m_ctx // bq
    past = cache_k.shape[1]
    ck = cache_k.reshape(cfg.n_lat, past, nkv * dh)
    cv = cache_v.reshape(cfg.n_lat, past, nkv * dh)

    def band_spec(delta):
        return pl.BlockSpec((bq, dh), lambda b, kv, qb: (off + b * nb + jnp.clip(qb + delta, 0, nb - 1), kv))

    cache_spec = pl.BlockSpec((None, past, dh), lambda b, kv, qb: (b, 0, kv))
    o_lat = pl.pallas_call(
        functools.partial(_attn_kernel, 3, nb),
        grid=(cfg.n_lat, nkv, nb),
        in_specs=[pl.BlockSpec((bq, ATTN_GROUP * dh), lambda b, kv, qb: (off + b * nb + qb, kv)),
                  band_spec(-1), band_spec(0), band_spec(1), cache_spec,
                  band_spec(-1), band_spec(0), band_spec(1), cache_spec,
                  sink_spec],
        out_specs=pl.BlockSpec((bq, ATTN_GROUP * dh), lambda b, kv, qb: (b * nb + qb, kv)),
        out_shape=jax.ShapeDtypeStruct((cfg.m - cfg.m_ctx, d), BF16),
        compiler_params=_cparams(3),
        name="attn_latent",
    )(q, k, k, k, ck, v, v, v, cv, sink_col)

    o = jnp.concatenate([o_ctx, o_lat], axis=0)
    x_new = _mm(o, p['w_o'], _w2_spec(d, 512), d, tm=tm, tn=512, out_dtype=F32, name="attn_wo",
                epi=_epi_resid, epi_args=(x, mod), epi_specs=(_tile_spec(tm, 512), _mod_spec2(cfg, l, 2, tm, 512)))
    new_k = k[:cfg.m_ctx].reshape(cfg.n_ctx, 1, cfg.l_ctx, nkv, dh)
    new_v = v[:cfg.m_ctx].reshape(cfg.n_ctx, 1, cfg.l_ctx, nkv, dh)
    return x_new, new_k, new_v


def _ffn_up_kernel(cfg, x_ref, wg_ref, wv_ref, cwg_ref, cwv_ref, cbg_ref, cbv_ref, o_ref):
    i = pl.program_id(0)
    tm = x_ref.shape[0]
    seq_len = _seq_len_of_tile(cfg, i, tm)
    pos = lax.broadcasted_iota(jnp.int32, (tm, 1), 0) & (seq_len - 1)
    x = x_ref[...]

    def conv(w_ref, cw_ref, cb_ref):
        u = jnp.dot(x, w_ref[...].astype(BF16), preferred_element_type=F32)
        prev = _shift_rows(u, pos, seq_len, 1)
        nxt = _shift_rows(u, pos, seq_len, -1)
        return prev * cw_ref[0:1, :] + u * cw_ref[1:2, :] + nxt * cw_ref[2:3, :] + cb_ref[...]

    gate = conv(wg_ref, cwg_ref, cbg_ref)
    val = conv(wv_ref, cwv_ref, cbv_ref)
    o_ref[...] = (gate * jax.nn.sigmoid(gate) * val).astype(o_ref.dtype)


def _conv_ffn(cfg, h, x, up, conv_w, conv_b, down, mod, l):
    d, tm = cfg.d, cfg.tm
    f = down.shape[1]
    tn = 256
    nj = f // tn
    cw = lambda off: pl.BlockSpec((None, conv_w.shape[1], tn), lambda i, j: (l, 0, j + off))
    cb = lambda off: pl.BlockSpec((None, 1, tn), lambda i, j: (l, 0, j + off))
    act = pl.pallas_call(
        functools.partial(_ffn_up_kernel, cfg),
        grid=(cfg.m // tm, nj),
        in_specs=[pl.BlockSpec((tm, d), lambda i, j: (i, 0)),
                  _w3_spec(l, d, tn), _w3_spec(l, d, tn, nj),
                  cw(0), cw(nj), cb(0), cb(nj)],
        out_specs=_tile_spec(tm, tn),
        out_shape=jax.ShapeDtypeStruct((cfg.m, f), BF16),
        compiler_params=_cparams(2),
        name="ffn_up",
    )(h, up, up, conv_w, conv_w, conv_b.reshape(conv_b.shape[0], 1, 2 * f), conv_b.reshape(conv_b.shape[0], 1, 2 * f))
    tmd = min(1024, tm)
    return _mm(act, down, _w3_spec(l, f, 256), d, tm=tmd, tn=256, out_dtype=F32, name="ffn_down",
               epi=_epi_resid, epi_args=(x, mod), epi_specs=(_tile_spec(tmd, 256), _mod_spec2(cfg, l, 5, tmd, 256)))


def kernel(x_prompt, x_sample, state_rwkv, cache_k, cache_v, c, c_ctx, ada_w, ada_b, norm_mix, norm_ffn, ffn_up, ffn_conv_w, ffn_conv_b, ffn_down, pool_w, pool_scale, rwkv_mu, rwkv_w_r, rwkv_w_k, rwkv_w_v, rwkv_w0, rwkv_w1, rwkv_w2, rwkv_a0, rwkv_a1, rwkv_a2, rwkv_g1, rwkv_g2, rwkv_k_k, rwkv_k_a, rwkv_r_k, rwkv_ln_w, rwkv_ln_b, rwkv_w_o, attn_w_qkv, attn_q_norm, attn_k_norm, attn_sink, attn_w_o):
    n_ctx, l_ctx, d = x_prompt.shape
    n_lat, l_lat, _ = x_sample.shape
    depth = ada_w.shape[0]
    m_ctx = n_ctx * l_ctx
    m = m_ctx + n_lat * l_lat
    cfg = _Cfg(n_ctx, l_ctx, n_lat, l_lat, d, m_ctx, m, l_lat)
    assert l_lat % l_ctx == 0 and m_ctx % l_lat == 0, "row tiles must hold whole sequences"
    assert l_ctx & (l_ctx - 1) == 0 and l_lat & (l_lat - 1) == 0, "sequence lengths must be powers of two"

    x = jnp.concatenate([x_prompt.reshape(m_ctx, d), x_sample.reshape(m - m_ctx, d)], axis=0)

    n_cond = 1 + n_lat
    ncp = -(-n_cond // V7X_SUBLANES) * V7X_SUBLANES
    cond = jnp.concatenate([c_ctx[None, :], c, jnp.zeros((ncp - n_cond, d), F32)], axis=0)
    mod = _adaln(cond, ada_w, ada_b)
    mod = mod.reshape(depth, ncp, 6, 1, d).transpose(0, 2, 1, 3, 4)

    n_mixers = 3
    slots = [0] * n_mixers
    new_state = new_k = new_v = None
    for l in range(depth):
        kind = l % n_mixers
        slot = slots[kind]
        slots[kind] += 1
        if kind == 0:
            h = _norm_mod(cfg, x, norm_mix, mod, l, 0, 1, F32)
            x = _pool_mixer(cfg, h, x, pool_w[slot], pool_scale[slot], mod, l)
        elif kind == 1:
            h = _norm_mod(cfg, x, norm_mix, mod, l, 0, 1, F32)
            p = {'mu': rwkv_mu[slot], 'w_r': rwkv_w_r[slot], 'w_k': rwkv_w_k[slot], 'w_v': rwkv_w_v[slot],
                 'w0': rwkv_w0[slot], 'w1': rwkv_w1[slot], 'w2': rwkv_w2[slot],
                 'a0': rwkv_a0[slot], 'a1': rwkv_a1[slot], 'a2': rwkv_a2[slot],
                 'g1': rwkv_g1[slot], 'g2': rwkv_g2[slot], 'k_k': rwkv_k_k[slot], 'k_a': rwkv_k_a[slot],
                 'r_k': rwkv_r_k[slot], 'ln_w': rwkv_ln_w[slot], 'ln_b': rwkv_ln_b[slot], 'w_o': rwkv_w_o[slot]}
            x, st = _rwkv_mixer(cfg, h, x, p, state_rwkv[:, slot], mod, l)
            new_state = st[:, None] if new_state is None else jnp.concatenate([new_state, st[:, None]], axis=1)
        else:
            h = _norm_mod(cfg, x, norm_mix, mod, l, 0, 1, BF16)
            p = {'w_qkv': attn_w_qkv[slot], 'q_norm': attn_q_norm[slot], 'k_norm': attn_k_norm[slot],
                 'sink': attn_sink[slot], 'w_o': attn_w_o[slot]}
            x, nk, nv = _attn_mixer(cfg, h, x, p, cache_k[:, slot], cache_v[:, slot], mod, l)
            new_k = nk if new_k is None else jnp.concatenate([new_k, nk], axis=1)
            new_v = nv if new_v is None else jnp.concatenate([new_v, nv], axis=1)
        h2 = _norm_mod(cfg, x, norm_ffn, mod, l, 3, 4, BF16)
        x = _conv_ffn(cfg, h2, x, ffn_up, ffn_conv_w, ffn_conv_b, ffn_down, mod, l)

    y_prompt = x[:m_ctx].reshape(n_ctx, l_ctx, d)
    y_sample = x[m_ctx:].reshape(n_lat, l_lat, d)
    return (y_prompt, y_sample, new_state, new_k, new_v)
```

```python
import functools
from typing import Callable, NamedTuple

import jax
import jax.numpy as jnp
from jax import lax
from jax.experimental import pallas as pl
from jax.experimental.pallas import tpu as pltpu

F32 = jnp.float32
BF16 = jnp.bfloat16

V7X_LANES = 128
V7X_SUBLANES = 8
V7X_BF16_ROWS = 16
V7X_VMEM_LIMIT_BYTES = 56 * 1024 * 1024

NORM_EPS = 1e-6
GN_EPS = 64e-5
DECAY_SCALE = 0.606531
POOL_WINDOWS = (2, 4, 8, 16)
RWKV_HEAD = 64
ATTN_HEAD_DIM = 128
ATTN_GROUP = 4
ATTN_BLOCK = 128
GRID_W = 64
ROPE_BASE = 10000.0
ATTN_SCALE = ATTN_HEAD_DIM ** -0.5
NEG_INF = -1e30
LORA_PAD = 128

TM_FFN = 2048
TM_MM = 1024
TM_ROW = 512
SCAN_TB = 16


class _Seg(NamedTuple):
    nb: int
    t: int
    cond0: int
    per_seq_cond: bool

    @property
    def m(self):
        return self.nb * self.t


class _Tiling(NamedTuple):
    n_i: int
    tm: int
    view: Callable
    unview: Callable
    vshape: Callable
    full: Callable
    tile: Callable
    cond: Callable


def _rows_tiling(seg, tm):
    tm = min(tm, seg.m)
    return _Tiling(seg.m // tm, tm, lambda a: a, lambda a: a, lambda w: (seg.m, w),
                   lambda w: pl.BlockSpec((tm, w), lambda i, j: (i, 0)),
                   lambda w, tn: pl.BlockSpec((tm, tn), lambda i, j: (i, j)),
                   lambda i: seg.cond0)


def _seq_tiling(seg, tt):
    tt = min(tt, seg.t)
    nti = seg.t // tt
    return _Tiling(seg.nb * nti, tt,
                   lambda a: a.reshape(seg.t, seg.nb * a.shape[1]),
                   lambda a: a.reshape(seg.m, a.shape[1] // seg.nb),
                   lambda w: (seg.t, seg.nb * w),
                   lambda w: pl.BlockSpec((tt, w), lambda i, j: (i % nti, i // nti)),
                   lambda w, tn: pl.BlockSpec((tt, tn), lambda i, j: (i % nti, (i // nti) * (w // tn) + j)),
                   lambda i: seg.cond0 + (i // nti if seg.per_seq_cond else 0))


def _cond_tiling(seg, tm):
    return _seq_tiling(seg, tm) if seg.per_seq_cond else _rows_tiling(seg, tm)


def _batch_major_spec(seg, til, tn):
    nti = seg.t // til.tm
    return pl.BlockSpec((None, til.tm, tn), lambda i, j: (i // nti, i % nti, j))


def _cparams(n_axes):
    return pltpu.CompilerParams(dimension_semantics=("arbitrary",) * n_axes,
                                vmem_limit_bytes=V7X_VMEM_LIMIT_BYTES)


def _mod_spec(til, l, which, tn):
    return pl.BlockSpec((None, None, None, 1, tn), lambda i, j: (l, which, til.cond(i), 0, j))


def _halo_specs(til, rows, width_block, col_fn):
    per = til.tm // rows
    n_blocks = til.n_i * per
    prev = pl.BlockSpec((rows, width_block), lambda i, j: (jnp.maximum(i * per - 1, 0), col_fn(j)))
    nxt = pl.BlockSpec((rows, width_block), lambda i, j: (jnp.minimum((i + 1) * per, n_blocks - 1), col_fn(j)))
    return prev, nxt


def _with_halo(x_ref, prev_ref, next_ref, n_i):
    i = pl.program_id(0)
    prev = jnp.where(i > 0, prev_ref[...], jnp.zeros_like(prev_ref[...]))
    nxt = jnp.where(i < n_i - 1, next_ref[...], jnp.zeros_like(next_ref[...]))
    return jnp.concatenate([prev, x_ref[...], nxt], axis=0)


def _adaln_kernel(c_ref, w_ref, b_ref, o_ref):
    c = c_ref[...]
    s = c * jax.nn.sigmoid(c)
    acc = jnp.dot(s.astype(BF16), w_ref[...].astype(BF16), preferred_element_type=F32)
    o_ref[...] = acc + b_ref[...]


def _adaln(cond, ada_w, ada_b):
    depth, d, n6 = ada_w.shape
    ncp = cond.shape[0]
    tn = 1024
    return pl.pallas_call(
        _adaln_kernel,
        grid=(depth, n6 // tn),
        in_specs=[pl.BlockSpec((ncp, d), lambda l, j: (0, 0)),
                  pl.BlockSpec((None, d, tn), lambda l, j: (l, 0, j)),
                  pl.BlockSpec((None, 1, tn), lambda l, j: (l, 0, j))],
        out_specs=pl.BlockSpec((None, ncp, tn), lambda l, j: (l, 0, j)),
        out_shape=jax.ShapeDtypeStruct((depth, ncp, n6), F32),
        compiler_params=_cparams(2),
        name="adaln",
    )(cond, ada_w, ada_b.reshape(depth, 1, n6))


def _norm_mod_kernel(x_ref, g_ref, sh_ref, sc_ref, o_ref, *copy_ref):
    x = x_ref[...]
    y = x * lax.rsqrt(jnp.mean(x * x, axis=-1, keepdims=True) + NORM_EPS)
    y = y * g_ref[...]
    o_ref[...] = (y * (1.0 + sc_ref[...]) + sh_ref[...]).astype(o_ref.dtype)
    if copy_ref:
        copy_ref[0][...] = x


def _norm_mod(seg, x, g, mod5, l, shift_i, scale_i, out_dtype, batch_major_in=False):
    d = g.shape[-1]
    if batch_major_in:
        til = _seq_tiling(seg, TM_ROW)
        x_arg, x_spec = x, _batch_major_spec(seg, til, d)
    else:
        til = _cond_tiling(seg, TM_ROW)
        x_arg, x_spec = til.view(x), til.full(d)
    out_sds = lambda dt: jax.ShapeDtypeStruct(til.vshape(d), dt)
    n_out = 2 if batch_major_in else 1
    outs = pl.pallas_call(
        _norm_mod_kernel,
        grid=(til.n_i, 1),
        in_specs=[x_spec,
                  pl.BlockSpec((None, 1, d), lambda i, j: (l, 0, 0)),
                  _mod_spec(til, l, shift_i, d), _mod_spec(til, l, scale_i, d)],
        out_specs=[til.full(d)] * n_out,
        out_shape=[out_sds(out_dtype), out_sds(F32)][:n_out],
        compiler_params=_cparams(2),
        name="norm_mod",
    )(x_arg, g.reshape(g.shape[0], 1, d), mod5, mod5)
    outs = [til.unview(o) for o in outs]
    return outs if batch_major_in else outs[0]


def _mm_kernel(n_pro, n_epi, pro, epi, x_ref, *refs):
    pro_refs = refs[:n_pro]
    w_ref = refs[n_pro]
    epi_refs = refs[n_pro + 1:n_pro + 1 + n_epi]
    o_ref = refs[n_pro + 1 + n_epi]
    x = x_ref[...]
    if pro is not None:
        x = pro(x, *pro_refs)
    acc = jnp.dot(x.astype(BF16), w_ref[...].astype(BF16), preferred_element_type=F32)
    if epi is not None:
        acc = epi(acc, *epi_refs)
    o_ref[...] = acc.astype(o_ref.dtype)


def _mm(seg, x, w, w_spec, n, *, tn, out_dtype, name, tm=None, cond=False, pro=None, pro_args=(),
        epi=None, epi_args=(), batch_major_out=False):
    k = x.shape[1]
    tm = TM_MM if tm is None else tm
    til = _cond_tiling(seg, tm) if (cond or batch_major_out) else _rows_tiling(seg, tm)
    if batch_major_out:
        til = _seq_tiling(seg, tm)

    def lower(arg):
        kind = arg[0]
        if kind == 'full':
            return til.view(arg[1]), til.full(arg[1].shape[1])
        if kind == 'tile':
            return til.view(arg[1]), til.tile(n, tn)
        if kind == 'row':
            return arg[1], pl.BlockSpec((1, tn), lambda i, j: (0, j))
        if kind == 'mod':
            return arg[1], _mod_spec(til, arg[2], arg[3], tn)
        return arg[1], arg[2]

    pro_l = [lower(a) for a in pro_args]
    epi_l = [lower(a) for a in epi_args]
    if batch_major_out:
        out_spec = _batch_major_spec(seg, til, tn)
        out_shape = jax.ShapeDtypeStruct((seg.nb, seg.t, n), out_dtype)
    else:
        out_spec = til.tile(n, tn)
        out_shape = jax.ShapeDtypeStruct(til.vshape(n), out_dtype)
    out = pl.pallas_call(
        functools.partial(_mm_kernel, len(pro_l), len(epi_l), pro, epi),
        grid=(til.n_i, n // tn),
        in_specs=[til.full(k), *[s for _, s in pro_l], w_spec, *[s for _, s in epi_l]],
        out_specs=out_spec,
        out_shape=out_shape,
        compiler_params=_cparams(2),
        name=name,
    )(til.view(x), *[a for a, _ in pro_l], w, *[a for a, _ in epi_l])
    return out if batch_major_out else til.unview(out)


def _w2_spec(k, tn, joff=0):
    return pl.BlockSpec((k, tn), lambda i, j: (0, j + joff))


def _w3_spec(l, k, tn, joff=0):
    return pl.BlockSpec((None, k, tn), lambda i, j: (l, 0, j + joff))


def _epi_resid(acc, x_ref, gate_ref):
    return x_ref[...] + gate_ref[...] * acc


def _epi_sigmoid(acc):
    return jax.nn.sigmoid(acc)


def _epi_tanh(acc):
    return jnp.tanh(acc)


def _epi_decay(acc, w0_ref):
    return jnp.exp(-DECAY_SCALE * jax.nn.sigmoid(w0_ref[...] + acc))


def _epi_bias_sigmoid(acc, a0_ref):
    return jax.nn.sigmoid(a0_ref[...] + acc)


def _pro_mul(x, g_ref):
    return x * g_ref[...]


def _pool_kernel(seg, n_i, h_ref, hp_ref, hn_ref, x_ref, w_ref, sc_ref, gate_ref, o_ref):
    i = pl.program_id(0)
    g = pl.program_id(1)
    tm = h_ref.shape[0]
    nb = seg.nb
    halo = hp_ref.shape[0]
    steps = tm // nb
    t_abs = i * steps + lax.broadcasted_iota(jnp.int32, (tm, 1), 0) // nb

    for gi, win in enumerate(POOL_WINDOWS):
        @pl.when(g == gi)
        def _(win=win):
            left = win // 2
            right = win - 1 - left
            ext = _with_halo(h_ref, hp_ref, hn_ref, n_i)
            acc, width = ext, 1
            while width < win:
                shift = width * nb
                acc = acc[shift:] + acc[:acc.shape[0] - shift]
                width *= 2
            start = halo - left * nb
            total = acc[start:start + tm]
            cnt = (jnp.minimum(t_abs + right + 1, seg.t) - jnp.maximum(t_abs - left, 0)).astype(F32)
            h = h_ref[...]
            pooled = total / cnt - h
            y = jnp.dot(pooled.astype(BF16), w_ref[...].astype(BF16), preferred_element_type=F32) * sc_ref[...]
            gate = gate_ref[...]
            if gate.shape[0] > 1:
                y = (y.reshape(tm // nb, nb, y.shape[1]) * gate[None]).reshape(tm, y.shape[1])
            else:
                y = y * gate
            o_ref[...] = x_ref[...] + y


def _pool_mixer(seg, h, x, w_grp, scale, mod4, l):
    ngrp, gsz, _ = w_grp.shape
    d = h.shape[1]
    til = _rows_tiling(seg, TM_MM)
    halo = max(POOL_WINDOWS) // 2 * seg.nb
    hp, hn = _halo_specs(til, halo, gsz, lambda j: j)
    if seg.per_seq_cond:
        gate_spec = pl.BlockSpec((None, None, seg.nb, gsz), lambda i, j: (l, 2, seg.cond0 // seg.nb, j))
    else:
        gate_spec = pl.BlockSpec((None, None, 1, gsz), lambda i, j: (l, 2, 0, j))
        mod4 = mod4[:, :, seg.cond0:seg.cond0 + 1]
    return pl.pallas_call(
        functools.partial(_pool_kernel, seg, til.n_i),
        grid=(til.n_i, ngrp),
        in_specs=[til.tile(d, gsz), hp, hn, til.tile(d, gsz),
                  pl.BlockSpec((None, gsz, gsz), lambda i, j: (j, 0, 0)),
                  pl.BlockSpec((1, gsz), lambda i, j: (0, j)),
                  gate_spec],
        out_specs=til.tile(d, gsz),
        out_shape=jax.ShapeDtypeStruct((seg.m, d), F32),
        compiler_params=_cparams(2),
        name="pool_mixer",
    )(h, h, h, x, w_grp, scale.reshape(1, d), mod4)


def _shift_mix_kernel(nb, n_i, h_ref, hp_ref, hn_ref, mu_ref, *o_refs):
    tm = h_ref.shape[0]
    ext = _with_halo(h_ref, hp_ref, hn_ref, n_i)
    h = h_ref[...]
    xx = 0.5 * (ext[0:tm] + ext[2 * nb:2 * nb + tm]) - h
    for n, o_ref in enumerate(o_refs):
        o_ref[...] = (h + xx * mu_ref[n:n + 1, :]).astype(o_ref.dtype)


def _shift_mix(seg, h, mu):
    d = h.shape[1]
    tn = 512
    n_out = mu.shape[0]
    til = _rows_tiling(seg, TM_MM)
    hp, hn = _halo_specs(til, seg.nb, tn, lambda j: j)
    return pl.pallas_call(
        functools.partial(_shift_mix_kernel, seg.nb, til.n_i),
        grid=(til.n_i, d // tn),
        in_specs=[til.tile(d, tn), hp, hn, pl.BlockSpec((n_out, tn), lambda i, j: (0, j))],
        out_specs=[til.tile(d, tn)] * n_out,
        out_shape=[jax.ShapeDtypeStruct((seg.m, d), BF16)] * n_out,
        compiler_params=_cparams(2),
        name="rwkv_shift_mix",
    )(h, h, h, mu)


def _scan_kernel(tb_steps, r_ref, k_ref, v_ref, w_ref, ag_ref, kk_ref, ka_ref, rk_ref, s0_ref,
                 o_ref, bonus_ref, s_ref, r_s, v_s, w_s, a_s, b_s, kd_s):
    e = pl.program_id(0)
    tb = pl.program_id(2)
    n = RWKV_HEAD
    nvb = n // V7X_SUBLANES

    @pl.when(tb == 0)
    def _():
        s_ref[...] = s0_ref[...]

    def prep(t, carry):
        k = k_ref[t].T.reshape(2, n, V7X_LANES)
        ag = ag_ref[t].T.reshape(2, n, V7X_LANES)
        r = r_ref[t].T.reshape(2, n, V7X_LANES)
        kk = k * kk_ref[...]
        kk = kk / jnp.maximum(jnp.sqrt(jnp.sum(kk * kk, axis=1, keepdims=True)), 1e-12)
        kd = k * (1.0 + (ag - 1.0) * ka_ref[...])
        r_s[t] = r
        v_s[t] = v_ref[t].T.reshape(2, n, V7X_LANES)
        w_s[t] = w_ref[t].T.reshape(2, n, V7X_LANES)
        a_s[t] = -kk
        b_s[t] = kk * ag
        kd_s[t] = kd
        bonus_ref[t] = jnp.sum(r * kd * rk_ref[...], axis=1)
        return carry

    lax.fori_loop(0, tb_steps, prep, 0)

    def vslice(vb):
        return pl.ds(vb * V7X_SUBLANES, V7X_SUBLANES)

    zeros = tuple(jnp.zeros((V7X_SUBLANES, V7X_LANES), F32) for _ in range(nvb))

    def step(s, carry):
        t = jnp.where(e == 0, s, tb_steps - 1 - s)
        for par in range(2):
            def pass1(kb, acc):
                acc = list(acc)
                for kr in range(V7X_SUBLANES):
                    ki = kb * V7X_SUBLANES + kr
                    a_row = a_s[t, par, pl.ds(ki, 1), :]
                    for vb in range(nvb):
                        acc[vb] = acc[vb] + s_ref[par, ki, vslice(vb), :] * a_row
                return tuple(acc)

            sa = lax.fori_loop(0, n // V7X_SUBLANES, pass1, zeros)
            vv = [v_s[t, par, vslice(vb), :] for vb in range(nvb)]

            def pass2(kb, acc):
                acc = list(acc)
                for kr in range(V7X_SUBLANES):
                    ki = kb * V7X_SUBLANES + kr
                    w_row = w_s[t, par, pl.ds(ki, 1), :]
                    b_row = b_s[t, par, pl.ds(ki, 1), :]
                    k_row = kd_s[t, par, pl.ds(ki, 1), :]
                    r_row = r_s[t, par, pl.ds(ki, 1), :]
                    for vb in range(nvb):
                        sn = s_ref[par, ki, vslice(vb), :] * w_row + sa[vb] * b_row + vv[vb] * k_row
                        s_ref[par, ki, vslice(vb), :] = sn
                        acc[vb] = acc[vb] + sn * r_row
                return tuple(acc)

            o = lax.fori_loop(0, n // V7X_SUBLANES, pass2, zeros)
            for vb in range(nvb):
                o_ref[t, par, vslice(vb), :] = o[vb]
        return carry

    lax.fori_loop(0, tb_steps, step, 0)


def _rwkv_scan(seg, r, k, v, w2, ag2, kk_p, ka_p, rk_p, s0):
    n, lanes = RWKV_HEAD, V7X_LANES
    t = seg.t
    g = seg.nb // V7X_SUBLANES
    tb = min(SCAN_TB, t)
    nt = t // tb
    rows = seg.nb * r.shape[1] // lanes
    assert rows == g * lanes

    def tmap(e, tt):
        return jnp.where(e == 0, tt, nt - 1 - tt)

    shared = pl.BlockSpec((tb, lanes, lanes), lambda e, gg, tt: (tmap(e, tt), gg, 0))
    per_dir = pl.BlockSpec((None, tb, lanes, lanes), lambda e, gg, tt: (e, tmap(e, tt), gg, 0))
    par = pl.BlockSpec((2, n, lanes), lambda e, gg, tt: (0, 0, 0))
    st = pl.BlockSpec((None, None, 2, n, n, lanes), lambda e, gg, tt: (e, gg, 0, 0, 0, 0))
    o_spec = pl.BlockSpec((None, None, tb, 2, n, lanes), lambda e, gg, tt: (e, gg, tmap(e, tt), 0, 0, 0))
    bon_spec = pl.BlockSpec((None, None, tb, 2, lanes), lambda e, gg, tt: (e, gg, tmap(e, tt), 0, 0))
    v3 = lambda a: a.reshape(t, rows, lanes)
    v4 = lambda a: a.reshape(2, t, rows, lanes)
    scratch = pltpu.VMEM((tb, 2, n, lanes), F32)
    return pl.pallas_call(
        functools.partial(_scan_kernel, tb),
        grid=(2, g, nt),
        in_specs=[shared, shared, shared, per_dir, per_dir, par, par, par, st],
        out_specs=[o_spec, bon_spec, st],
        out_shape=[jax.ShapeDtypeStruct((2, g, t, 2, n, lanes), F32),
                   jax.ShapeDtypeStruct((2, g, t, 2, lanes), F32),
                   jax.ShapeDtypeStruct((2, g, 2, n, n, lanes), F32)],
        scratch_shapes=[scratch] * 6,
        compiler_params=_cparams(3),
        name="rwkv_scan",
    )(v3(r), v3(k), v3(v), v4(w2), v4(ag2), kk_p, ka_p, rk_p, s0)


def _rwkv_post_kernel(tb_steps, of_ref, ob_ref, v_ref, bf_ref, bb_ref, lnw_ref, lnb_ref, y_ref):
    n = RWKV_HEAD

    def body(t, carry):
        o = of_ref[t] + ob_ref[t]
        mean = jnp.mean(o, axis=1, keepdims=True)
        var = jnp.mean(jnp.square(o - mean), axis=1, keepdims=True)
        on = (o - mean) * lax.rsqrt(var + GN_EPS)
        v = v_ref[t].T.reshape(2, n, V7X_LANES)
        bonus = (bf_ref[t] + bb_ref[t])[:, None, :]
        y = on * lnw_ref[...] + lnb_ref[...] + bonus * v
        y_ref[t] = y.reshape(2 * n, V7X_LANES).T
        return carry

    lax.fori_loop(0, tb_steps, body, 0)


def _rwkv_post(seg, o2, v, bonus2, lnw_p, lnb_p):
    _, g, t, _, n, lanes = o2.shape
    tb = min(32, t)
    rows = g * lanes
    blk = lambda e: pl.BlockSpec((None, None, tb, 2, n, lanes), lambda gg, tt: (e, gg, tt, 0, 0, 0))
    bon = lambda e: pl.BlockSpec((None, None, tb, 2, lanes), lambda gg, tt: (e, gg, tt, 0, 0))
    par = pl.BlockSpec((2, n, lanes), lambda gg, tt: (0, 0, 0))
    nat = pl.BlockSpec((tb, lanes, lanes), lambda gg, tt: (tt, gg, 0))
    y = pl.pallas_call(
        functools.partial(_rwkv_post_kernel, tb),
        grid=(g, t // tb),
        in_specs=[blk(0), blk(1), nat, bon(0), bon(1), par, par],
        out_specs=nat,
        out_shape=jax.ShapeDtypeStruct((t, rows, lanes), F32),
        compiler_params=_cparams(2),
        name="rwkv_post",
    )(o2, o2, v.reshape(t, rows, lanes), bonus2, bonus2, lnw_p, lnb_p)
    return y.reshape(seg.m, v.shape[1])


def _param_lanes(p):
    h = p.size // RWKV_HEAD
    q = p.reshape(h // 2, 2, RWKV_HEAD).transpose(1, 2, 0)
    return jnp.tile(q, (1, 1, V7X_LANES // (h // 2)))


def _pad_lora_in(w):
    r = w.shape[-1]
    w = jnp.pad(w, ((0, 0), (0, 0), (0, LORA_PAD - r)))
    return jnp.concatenate([w[0], w[1]], axis=1)


def _pad_lora_out(w):
    r = w.shape[1]
    z = jnp.zeros((LORA_PAD, w.shape[2]), w.dtype)
    wp = jnp.pad(w, ((0, 0), (0, LORA_PAD - r), (0, 0)))
    return jnp.stack([jnp.concatenate([wp[0], z], axis=0), jnp.concatenate([z, wp[1]], axis=0)])


def _dir_mm(seg, x, w2, bias2, epi, name):
    k = x.shape[1]
    d = w2.shape[2]
    til = _rows_tiling(seg, TM_MM)
    tn = 512
    return pl.pallas_call(
        functools.partial(_mm_kernel, 0, 1, None, epi),
        grid=(2 * til.n_i, d // tn),
        in_specs=[pl.BlockSpec((til.tm, k), lambda i, j: (i % til.n_i, 0)),
                  pl.BlockSpec((None, k, tn), lambda i, j: (i // til.n_i, 0, j)),
                  pl.BlockSpec((None, 1, tn), lambda i, j: (i // til.n_i, 0, j))],
        out_specs=pl.BlockSpec((None, til.tm, tn), lambda i, j: (i // til.n_i, i % til.n_i, j)),
        out_shape=jax.ShapeDtypeStruct((2, seg.m, d), F32),
        compiler_params=_cparams(2),
        name=name,
    )(x, w2, bias2.reshape(2, 1, d))


def _rwkv_mixer(seg, h, x, p, state, mod5, l):
    d = h.shape[1]
    xr, xw, xk, xv, xa, xg = _shift_mix(seg, h, p['mu'])
    mmf = functools.partial(_mm, seg, tn=512)
    r = mmf(xr, p['w_r'], _w2_spec(d, 512), d, out_dtype=F32, name="rwkv_r")
    k = mmf(xk, p['w_k'], _w2_spec(d, 512), d, out_dtype=F32, name="rwkv_k")
    v = mmf(xv, p['w_v'], _w2_spec(d, 512), d, out_dtype=F32, name="rwkv_v")
    gl = _mm(seg, xg, p['g1'], _w2_spec(d, 256), 256, tn=256, out_dtype=BF16, name="rwkv_g1", epi=_epi_sigmoid)
    gate = mmf(gl, p['g2'], _w2_spec(256, 512), d, out_dtype=F32, name="rwkv_g2")
    nl = 2 * LORA_PAD
    wl = _mm(seg, xw, p['w1p'], _w2_spec(d, nl), nl, tn=nl, out_dtype=BF16, name="rwkv_w1", epi=_epi_tanh)
    al = _mm(seg, xa, p['a1p'], _w2_spec(d, nl), nl, tn=nl, out_dtype=BF16, name="rwkv_a1")
    dec2 = _dir_mm(seg, wl, p['w2p'], p['w0'], _epi_decay, "rwkv_w2")
    ag2 = _dir_mm(seg, al, p['a2p'], p['a0'], _epi_bias_sigmoid, "rwkv_a2")

    n_heads = d // RWKV_HEAD
    n = RWKV_HEAD
    g = seg.nb // V7X_SUBLANES
    if state is None:
        s0 = jnp.zeros((2, g, 2, n, n, V7X_LANES), F32)
    else:
        s0 = state.reshape(g, V7X_SUBLANES, 2, n_heads // 2, 2, n, n).transpose(2, 0, 4, 6, 5, 1, 3)
        s0 = s0.reshape(2, g, 2, n, n, V7X_LANES)
    o2, bonus2, sfin = _rwkv_scan(seg, r, k, v, dec2, ag2, p['kk_p'], p['ka_p'], p['rk_p'], s0)
    y = _rwkv_post(seg, o2, v, bonus2, p['lnw_p'], p['lnb_p'])
    x_new = _mm(seg, y, p['w_o'], _w2_spec(d, 512), d, tn=512, tm=TM_ROW, out_dtype=F32, name="rwkv_wo", cond=True,
                pro=_pro_mul, pro_args=(('full', gate),),
                epi=_epi_resid, epi_args=(('tile', x), ('mod', mod5, l, 2)))
    return x_new, sfin


def _head_rmsnorm(acc, w_ref):
    outs = []
    for hh in range(acc.shape[1] // ATTN_HEAD_DIM):
        xh = acc[:, hh * ATTN_HEAD_DIM:(hh + 1) * ATTN_HEAD_DIM]
        yh = xh * lax.rsqrt(jnp.mean(xh * xh, axis=-1, keepdims=True) + NORM_EPS)
        outs.append(yh * w_ref[...])
    return jnp.concatenate(outs, axis=1) if len(outs) > 1 else outs[0]


def _rope(y, cos_ref, sin_ref):
    nh = y.shape[1] // ATTN_HEAD_DIM
    cos = jnp.concatenate([cos_ref[...]] * nh, axis=1) if nh > 1 else cos_ref[...]
    sin = jnp.concatenate([sin_ref[...]] * nh, axis=1) if nh > 1 else sin_ref[...]
    width = y.shape[1]
    lane = lax.broadcasted_iota(jnp.int32, (1, width), 1)
    quarter = ATTN_HEAD_DIM // 4
    first = (lane & (2 * quarter - 1)) < quarter
    swapped = jnp.where(first, pltpu.roll(y, width - quarter, axis=1), pltpu.roll(y, quarter, axis=1))
    return y * cos + swapped * sin


def _epi_qk_rope(acc, nw_ref, cos_ref, sin_ref):
    return _rope(_head_rmsnorm(acc, nw_ref), cos_ref, sin_ref)


def _epi_qk(acc, nw_ref):
    return _head_rmsnorm(acc, nw_ref)


def _rope_tables(seg):
    t = seg.t
    row = (jnp.arange(t) // GRID_W).astype(F32)
    col = (jnp.arange(t) % GRID_W).astype(F32)
    n_freq = ATTN_HEAD_DIM // 4
    inv = ROPE_BASE ** (-jnp.arange(n_freq, dtype=F32) / n_freq)
    ar, ac = row[:, None] * inv, col[:, None] * inv
    cos = jnp.concatenate([jnp.cos(ar), jnp.cos(ar), jnp.cos(ac), jnp.cos(ac)], axis=1)
    sin = jnp.concatenate([-jnp.sin(ar), jnp.sin(ar), -jnp.sin(ac), jnp.sin(ac)], axis=1)
    return jnp.repeat(cos, seg.nb, axis=0), jnp.repeat(sin, seg.nb, axis=0)


def _attn_kernel(n_band, nb, q_ref, *refs):
    n_src = (len(refs) - 2) // 2
    k_refs = refs[:n_src]
    v_refs = refs[n_src:2 * n_src]
    sink_ref = refs[2 * n_src]
    o_ref = refs[2 * n_src + 1]
    blk = pl.program_id(2)
    bq = q_ref.shape[0]
    q = jnp.concatenate([q_ref[:, gq * ATTN_HEAD_DIM:(gq + 1) * ATTN_HEAD_DIM] for gq in range(ATTN_GROUP)],
                        axis=0).astype(BF16)
    scores = []
    for si in range(n_src):
        s = lax.dot_general(q, k_refs[si][...].astype(BF16), (((1,), (1,)), ((), ())),
                            preferred_element_type=F32) * ATTN_SCALE
        if si < n_band and n_band == 3 and si != 1:
            qi = lax.broadcasted_iota(jnp.int32, (ATTN_GROUP * bq, bq), 0) & (bq - 1)
            ki = lax.broadcasted_iota(jnp.int32, (ATTN_GROUP * bq, bq), 1)
            if si == 0:
                valid, present = ki >= qi, blk > 0
            else:
                valid, present = ki <= qi, blk < nb - 1
            s = jnp.where(valid, s, NEG_INF) + jnp.where(present, 0.0, NEG_INF)
        scores.append(s)
    sink = sink_ref[...]
    m = sink
    for s in scores:
        m = jnp.maximum(m, jnp.max(s, axis=-1, keepdims=True))
    ps = [jnp.exp(s - m) for s in scores]
    den = jnp.exp(sink - m)
    for p in ps:
        den = den + jnp.sum(p, axis=-1, keepdims=True)
    inv = 1.0 / den
    out = None
    for si in range(n_src):
        o = jnp.dot((ps[si] * inv).astype(BF16), v_refs[si][...].astype(BF16), preferred_element_type=F32)
        out = o if out is None else out + o
    for gq in range(ATTN_GROUP):
        o_ref[:, gq * ATTN_HEAD_DIM:(gq + 1) * ATTN_HEAD_DIM] = out[gq * bq:(gq + 1) * bq, :].astype(o_ref.dtype)


def _attn_mixer(seg, h, x, p, cache_k, cache_v, mod5, l):
    d = h.shape[1]
    dh = ATTN_HEAD_DIM
    nkv = p['w_qkv'].shape[1] // dh // (ATTN_GROUP + 2)
    nq = nkv * ATTN_GROUP
    tn = 512
    norm_arg = lambda w: ('raw', w.reshape(1, dh), pl.BlockSpec((1, dh), lambda i, j: (0, 0)))
    if cache_k is None:
        epi, extra = _epi_qk, ()
    else:
        cos, sin = _rope_tables(seg)
        tm_q = min(TM_MM, seg.m)
        tab = lambda a: ('raw', a, pl.BlockSpec((tm_q, dh), lambda i, j: (i, 0)))
        epi, extra = _epi_qk_rope, (tab(cos), tab(sin))
    q = _mm(seg, h, p['w_qkv'], _w2_spec(d, tn), nq * dh, tn=tn, out_dtype=BF16, name="attn_q", epi=epi,
            epi_args=(norm_arg(p['q_norm']),) + extra)
    k = _mm(seg, h, p['w_qkv'], _w2_spec(d, tn, nq * dh // tn), nkv * dh, tn=tn, out_dtype=F32, name="attn_k",
            epi=epi, epi_args=(norm_arg(p['k_norm']),) + extra)
    v = _mm(seg, h, p['w_qkv'], _w2_spec(d, tn, (nq + nkv) * dh // tn), nkv * dh, tn=tn, out_dtype=F32,
            name="attn_v")

    bq = ATTN_BLOCK
    nb = seg.t // bq
    sink_col = jnp.repeat(p['sink'].reshape(nkv, ATTN_GROUP), bq, axis=1).reshape(nkv, ATTN_GROUP * bq, 1)
    sink_spec = pl.BlockSpec((None, ATTN_GROUP * bq, 1), lambda b, kv, qb: (kv, 0, 0))
    qv = q.reshape(seg.t, seg.nb * nq * dh)
    kv_ = k.reshape(seg.t, seg.nb * nkv * dh)
    vv = v.reshape(seg.t, seg.nb * nkv * dh)
    q_spec = pl.BlockSpec((bq, ATTN_GROUP * dh), lambda b, kv, qb: (qb, b * nkv + kv))
    if cache_k is None:
        full = pl.BlockSpec((seg.t, dh), lambda b, kv, qb: (0, b * nkv + kv))
        in_specs = [q_spec, full, full, sink_spec]
        args = (qv, kv_, vv, sink_col)
        n_band = 0
    else:
        past = cache_k.shape[1]
        ck = cache_k.reshape(seg.nb, past, nkv * dh)
        cv = cache_v.reshape(seg.nb, past, nkv * dh)

        def band(delta):
            return pl.BlockSpec((bq, dh), lambda b, kv, qb: (jnp.clip(qb + delta, 0, nb - 1), b * nkv + kv))

        cache_spec = pl.BlockSpec((None, past, dh), lambda b, kv, qb: (b, 0, kv))
        in_specs = [q_spec, band(-1), band(0), band(1), cache_spec, band(-1), band(0), band(1), cache_spec, sink_spec]
        args = (qv, kv_, kv_, kv_, ck, vv, vv, vv, cv, sink_col)
        n_band = 3
    o = pl.pallas_call(
        functools.partial(_attn_kernel, n_band, nb),
        grid=(seg.nb, nkv, nb),
        in_specs=in_specs,
        out_specs=q_spec,
        out_shape=jax.ShapeDtypeStruct((seg.t, seg.nb * nq * dh), BF16),
        compiler_params=_cparams(3),
        name="attn_core",
    )(*args).reshape(seg.m, nq * dh)

    x_new = _mm(seg, o, p['w_o'], _w2_spec(d, 512), d, tn=512, out_dtype=F32, name="attn_wo", cond=True,
                epi=_epi_resid, epi_args=(('tile', x), ('mod', mod5, l, 2)))
    return x_new, k, v


def _ffn_up_kernel(nb, n_i, x_ref, xp_ref, xn_ref, wg_ref, wv_ref, cwg_ref, cwv_ref, cbg_ref, cbv_ref, o_ref):
    tm = x_ref.shape[0]
    halo = xp_ref.shape[0]
    xh = _with_halo(x_ref, xp_ref, xn_ref, n_i)

    def conv(w_ref, cw_ref, cb_ref):
        u = jnp.dot(xh, w_ref[...].astype(BF16), preferred_element_type=F32)
        prev = u[halo - nb:halo - nb + tm]
        cur = u[halo:halo + tm]
        nxt = u[halo + nb:halo + nb + tm]
        return prev * cw_ref[0:1, :] + cur * cw_ref[1:2, :] + nxt * cw_ref[2:3, :] + cb_ref[...]

    gate = conv(wg_ref, cwg_ref, cbg_ref)
    val = conv(wv_ref, cwv_ref, cbv_ref)
    o_ref[...] = (gate * jax.nn.sigmoid(gate) * val).astype(o_ref.dtype)


def _conv_ffn(seg, h, x, up, conv_w, conv_b, down, mod5, l, batch_major_out):
    d = h.shape[1]
    f = down.shape[1]
    tn = 256
    nj = f // tn
    til = _rows_tiling(seg, TM_FFN)
    halo = max(V7X_BF16_ROWS, seg.nb)
    hp, hn = _halo_specs(til, halo, d, lambda j: 0)
    cw = lambda off: pl.BlockSpec((None, conv_w.shape[1], tn), lambda i, j: (l, 0, j + off))
    cb = lambda off: pl.BlockSpec((None, 1, tn), lambda i, j: (l, 0, j + off))
    cb3 = conv_b.reshape(conv_b.shape[0], 1, 2 * f)
    act = pl.pallas_call(
        functools.partial(_ffn_up_kernel, seg.nb, til.n_i),
        grid=(til.n_i, nj),
        in_specs=[til.full(d), hp, hn, _w3_spec(l, d, tn), _w3_spec(l, d, tn, nj), cw(0), cw(nj), cb(0), cb(nj)],
        out_specs=til.tile(f, tn),
        out_shape=jax.ShapeDtypeStruct((seg.m, f), BF16),
        compiler_params=_cparams(2),
        name="ffn_up",
    )(h, h, h, up, up, conv_w, conv_w, cb3, cb3)
    return _mm(seg, act, down, _w3_spec(l, f, 256), d, tn=256, out_dtype=F32, name="ffn_down", cond=True,
               epi=_epi_resid, epi_args=(('tile', x), ('mod', mod5, l, 5)), batch_major_out=batch_major_out)


def kernel(x_prompt, x_sample, state_rwkv, cache_k, cache_v, c, c_ctx, ada_w, ada_b, norm_mix, norm_ffn, ffn_up, ffn_conv_w, ffn_conv_b, ffn_down, pool_w, pool_scale, rwkv_mu, rwkv_w_r, rwkv_w_k, rwkv_w_v, rwkv_w0, rwkv_w1, rwkv_w2, rwkv_a0, rwkv_a1, rwkv_a2, rwkv_g1, rwkv_g2, rwkv_k_k, rwkv_k_a, rwkv_r_k, rwkv_ln_w, rwkv_ln_b, rwkv_w_o, attn_w_qkv, attn_q_norm, attn_k_norm, attn_sink, attn_w_o):
    n_ctx, l_ctx, d = x_prompt.shape
    n_lat, l_lat, _ = x_sample.shape
    depth = ada_w.shape[0]
    assert n_ctx % V7X_SUBLANES == 0 and n_lat % V7X_SUBLANES == 0, "the scan packs 8 sequences per lane group"
    assert d // RWKV_HEAD // 2 * V7X_SUBLANES == V7X_LANES, "8 sequences x head pairs must fill the lanes"

    n_cond = n_lat + 1
    ncp = -(-n_cond // V7X_SUBLANES) * V7X_SUBLANES
    cond = jnp.concatenate([c, c_ctx[None, :], jnp.zeros((ncp - n_cond, d), F32)], axis=0)
    mod = _adaln(cond, ada_w, ada_b)
    mod4 = mod.reshape(depth, ncp, 6, d).transpose(0, 2, 1, 3)
    mod5 = mod4.reshape(depth, 6, ncp, 1, d)

    segs = (_Seg(n_ctx, l_ctx, n_lat, False), _Seg(n_lat, l_lat, 0, True))
    xs = [None, None]
    x_in = (x_prompt, x_sample)

    n_mixers = 3
    slots = [0] * n_mixers
    new_state, new_k, new_v = [], [], []
    for l in range(depth):
        kind = l % n_mixers
        slot = slots[kind]
        slots[kind] += 1
        last = l == depth - 1
        if kind == 1:
            p = {'mu': rwkv_mu[slot], 'w_r': rwkv_w_r[slot], 'w_k': rwkv_w_k[slot], 'w_v': rwkv_w_v[slot],
                 'w0': rwkv_w0[slot], 'w1p': _pad_lora_in(rwkv_w1[slot]), 'w2p': _pad_lora_out(rwkv_w2[slot]),
                 'a0': rwkv_a0[slot], 'a1p': _pad_lora_in(rwkv_a1[slot]), 'a2p': _pad_lora_out(rwkv_a2[slot]),
                 'g1': rwkv_g1[slot], 'g2': rwkv_g2[slot], 'w_o': rwkv_w_o[slot],
                 'kk_p': _param_lanes(rwkv_k_k[slot]), 'ka_p': _param_lanes(rwkv_k_a[slot]),
                 'rk_p': _param_lanes(rwkv_r_k[slot]), 'lnw_p': _param_lanes(rwkv_ln_w[slot]),
                 'lnb_p': _param_lanes(rwkv_ln_b[slot])}
        elif kind == 2:
            p = {'w_qkv': attn_w_qkv[slot], 'q_norm': attn_q_norm[slot], 'k_norm': attn_k_norm[slot],
                 'sink': attn_sink[slot], 'w_o': attn_w_o[slot]}
        for si, seg in enumerate(segs):
            h_dtype = BF16 if kind == 2 else F32
            if l == 0:
                h, x = _norm_mod(seg, x_in[si], norm_mix, mod5, l, 0, 1, h_dtype, batch_major_in=True)
            else:
                x = xs[si]
                h = _norm_mod(seg, x, norm_mix, mod5, l, 0, 1, h_dtype)
            if kind == 0:
                x = _pool_mixer(seg, h, x, pool_w[slot], pool_scale[slot], mod4, l)
            elif kind == 1:
                x, sfin = _rwkv_mixer(seg, h, x, p, state_rwkv[:, slot] if si == 1 else None, mod5, l)
                if si == 0:
                    g = seg.nb // V7X_SUBLANES
                    hp = d // RWKV_HEAD // 2
                    st = sfin.reshape(2, g, 2, RWKV_HEAD, RWKV_HEAD, V7X_SUBLANES, hp).transpose(1, 5, 0, 6, 2, 4, 3)
                    new_state.append(st.reshape(seg.nb, 2, 2 * hp, RWKV_HEAD, RWKV_HEAD))
            else:
                x, k, v = _attn_mixer(seg, h, x, p, cache_k[:, slot] if si == 1 else None,
                                      cache_v[:, slot] if si == 1 else None, mod5, l)
                if si == 0:
                    nkv = k.shape[1] // ATTN_HEAD_DIM
                    to_bm = lambda a: a.reshape(seg.t, seg.nb, nkv, ATTN_HEAD_DIM).transpose(1, 0, 2, 3)
                    new_k.append(to_bm(k))
                    new_v.append(to_bm(v))
            h2 = _norm_mod(seg, x, norm_ffn, mod5, l, 3, 4, BF16)
            xs[si] = _conv_ffn(seg, h2, x, ffn_up, ffn_conv_w, ffn_conv_b, ffn_down, mod5, l,
                               batch_major_out=last and seg.per_seq_cond)

    y_prompt = xs[0].reshape(l_ctx, n_ctx, d).transpose(1, 0, 2)
    y_sample = xs[1]
    return (y_prompt, y_sample, jnp.stack(new_state, axis=1), jnp.stack(new_k, axis=1), jnp.stack(new_v, axis=1))
```

```python
import functools
from typing import NamedTuple

import jax
import jax.numpy as jnp
from jax import lax
from jax.experimental import pallas as pl
from jax.experimental.pallas import tpu as pltpu

F32 = jnp.float32
BF16 = jnp.bfloat16

V7X_LANES = 128
V7X_SUBLANES = 8
V7X_BF16_ROWS = 16
V7X_VMEM_LIMIT_BYTES = 56 * 1024 * 1024

NORM_EPS = 1e-6
GN_EPS = 64e-5
DECAY_SCALE = 0.606531
POOL_WINDOWS = (2, 4, 8, 16)
RWKV_HEAD = 64
ATTN_HEAD_DIM = 128
ATTN_GROUP = 4
ATTN_BLOCK = 128
GRID_W = 64
ROPE_BASE = 10000.0
ATTN_SCALE = ATTN_HEAD_DIM ** -0.5
NEG_INF = -1e30
LORA_PAD = 128

TM_FFN = 2048
TM_MM = 1024
TM_ROW = 512
SCAN_ROWS = 128


class _Seg(NamedTuple):
    nb: int
    t: int
    mod_rows: int

    @property
    def m(self):
        return self.nb * self.t


def _cparams(n_axes):
    return pltpu.CompilerParams(dimension_semantics=("arbitrary",) * n_axes,
                                vmem_limit_bytes=V7X_VMEM_LIMIT_BYTES)


def _tile_rows(seg, target):
    return min(target, seg.m)


def _full_spec(tm, w):
    return pl.BlockSpec((tm, w), lambda i, j: (i, 0))


def _tile_spec(tm, tn):
    return pl.BlockSpec((tm, tn), lambda i, j: (i, j))


def _mod_spec(seg, l, which, tn):
    return pl.BlockSpec((None, None, seg.mod_rows, tn), lambda i, j: (l, which, 0, j))


def _per_row(fn, y, m):
    if m.shape[0] == 1:
        return fn(y, m)
    tm, w = y.shape
    return fn(y.reshape(tm // m.shape[0], m.shape[0], w), m[None]).reshape(tm, w)


def _halo_specs(tm, n_i, rows, width_block, col_fn):
    per = tm // rows
    n_blocks = n_i * per
    prev = pl.BlockSpec((rows, width_block), lambda i, j: (jnp.maximum(i * per - 1, 0), col_fn(j)))
    nxt = pl.BlockSpec((rows, width_block), lambda i, j: (jnp.minimum((i + 1) * per, n_blocks - 1), col_fn(j)))
    return prev, nxt


def _with_halo(x_ref, prev_ref, next_ref, n_i):
    i = pl.program_id(0)
    prev = jnp.where(i > 0, prev_ref[...], jnp.zeros_like(prev_ref[...]))
    nxt = jnp.where(i < n_i - 1, next_ref[...], jnp.zeros_like(next_ref[...]))
    return jnp.concatenate([prev, x_ref[...], nxt], axis=0)


def _adaln_kernel(c_ref, w_ref, b_ref, o_ref):
    c = c_ref[...]
    s = c * jax.nn.sigmoid(c)
    acc = jnp.dot(s.astype(BF16), w_ref[...].astype(BF16), preferred_element_type=F32)
    o_ref[...] = acc + b_ref[...]


def _adaln(cond, ada_w, ada_b):
    depth, d, n6 = ada_w.shape
    ncp = cond.shape[0]
    tn = 1024
    return pl.pallas_call(
        _adaln_kernel,
        grid=(depth, n6 // tn),
        in_specs=[pl.BlockSpec((ncp, d), lambda l, j: (0, 0)),
                  pl.BlockSpec((None, d, tn), lambda l, j: (l, 0, j)),
                  pl.BlockSpec((None, 1, tn), lambda l, j: (l, 0, j))],
        out_specs=pl.BlockSpec((None, ncp, tn), lambda l, j: (l, 0, j)),
        out_shape=jax.ShapeDtypeStruct((depth, ncp, n6), F32),
        compiler_params=_cparams(2),
        name="adaln",
    )(cond, ada_w, ada_b.reshape(depth, 1, n6))


def _norm_mod_math(x, g, sh, sc):
    y = x * lax.rsqrt(jnp.mean(x * x, axis=-1, keepdims=True) + NORM_EPS)
    y = y * g
    return _per_row(lambda a, b: a * (1.0 + b), y, sc), sh


def _norm_mod_kernel(x_ref, g_ref, sh_ref, sc_ref, o_ref):
    y, sh = _norm_mod_math(x_ref[...], g_ref[...], sh_ref[...], sc_ref[...])
    o_ref[...] = _per_row(lambda a, b: a + b, y, sh).astype(o_ref.dtype)


def _norm_mod(seg, x, g, mod, l, shift_i, scale_i, out_dtype):
    d = g.shape[-1]
    tm = _tile_rows(seg, TM_ROW)
    return pl.pallas_call(
        _norm_mod_kernel,
        grid=(seg.m // tm, 1),
        in_specs=[_full_spec(tm, d), pl.BlockSpec((None, 1, d), lambda i, j: (l, 0, 0)),
                  _mod_spec(seg, l, shift_i, d), _mod_spec(seg, l, scale_i, d)],
        out_specs=_full_spec(tm, d),
        out_shape=jax.ShapeDtypeStruct((seg.m, d), out_dtype),
        compiler_params=_cparams(2),
        name="norm_mod",
    )(x, g.reshape(g.shape[0], 1, d), mod, mod)


def _mm_kernel(n_pro, n_epi, pro, epi, x_ref, *refs):
    pro_refs = refs[:n_pro]
    w_ref = refs[n_pro]
    epi_refs = refs[n_pro + 1:n_pro + 1 + n_epi]
    o_ref = refs[n_pro + 1 + n_epi]
    x = x_ref[...]
    if pro is not None:
        x = pro(x, *pro_refs)
    acc = jnp.dot(x.astype(BF16), w_ref[...].astype(BF16), preferred_element_type=F32)
    if epi is not None:
        acc = epi(acc, *epi_refs)
    if len(o_ref.shape) == 3:
        hd = o_ref.shape[2]
        for hh in range(o_ref.shape[0]):
            o_ref[hh] = acc[:, hh * hd:(hh + 1) * hd].astype(o_ref.dtype)
    else:
        o_ref[...] = acc.astype(o_ref.dtype)


def _mm(seg, x, w, w_spec, n, *, tn, out_dtype, name, tm=None, k=None, x_spec=None, pro=None, pro_args=(),
        epi=None, epi_args=(), head_dim_out=None):
    k = x.shape[1] if k is None else k
    tm = _tile_rows(seg, TM_MM if tm is None else tm)

    def lower(arg):
        kind = arg[0]
        if kind == 'full':
            return arg[1], _full_spec(tm, arg[1].shape[1])
        if kind == 'tile':
            return arg[1], _tile_spec(tm, tn)
        if kind == 'row':
            return arg[1], pl.BlockSpec((1, tn), lambda i, j: (0, j))
        if kind == 'mod':
            return arg[1], _mod_spec(seg, arg[2], arg[3], tn)
        return arg[1], arg[2]

    pro_l = [lower(a) for a in pro_args]
    epi_l = [lower(a) for a in epi_args]
    if head_dim_out is None:
        out_spec = _tile_spec(tm, tn)
        out_shape = jax.ShapeDtypeStruct((seg.m, n), out_dtype)
    else:
        out_spec = pl.BlockSpec((tn // head_dim_out, tm, head_dim_out), lambda i, j: (j, i, 0))
        out_shape = jax.ShapeDtypeStruct((n // head_dim_out, seg.m, head_dim_out), out_dtype)
    return pl.pallas_call(
        functools.partial(_mm_kernel, len(pro_l), len(epi_l), pro, epi),
        grid=(seg.m // tm, n // tn),
        in_specs=[_full_spec(tm, k) if x_spec is None else x_spec, *[sp for _, sp in pro_l], w_spec,
                  *[sp for _, sp in epi_l]],
        out_specs=out_spec,
        out_shape=out_shape,
        compiler_params=_cparams(2),
        name=name,
    )(x, *[a for a, _ in pro_l], w, *[a for a, _ in epi_l])


def _w2_spec(k, tn, joff=0):
    return pl.BlockSpec((k, tn), lambda i, j: (0, j + joff))


def _w3_spec(l, k, tn, joff=0):
    return pl.BlockSpec((None, k, tn), lambda i, j: (l, 0, j + joff))


def _epi_resid(acc, x_ref, gate_ref):
    return x_ref[...] + _per_row(lambda a, b: a * b, acc, gate_ref[...])


def _epi_sigmoid(acc):
    return jax.nn.sigmoid(acc)


def _epi_tanh(acc):
    return jnp.tanh(acc)


def _epi_decay(acc, w0_ref):
    return jnp.exp(-DECAY_SCALE * jax.nn.sigmoid(w0_ref[...] + acc))


def _epi_bias_sigmoid(acc, a0_ref):
    return jax.nn.sigmoid(a0_ref[...] + acc)


def _pro_mul(x, g_ref):
    return x * g_ref[...]


def _pool_kernel(seg, n_i, h_ref, hp_ref, hn_ref, x_ref, w_ref, sc_ref, gate_ref, o_ref):
    i = pl.program_id(0)
    g = pl.program_id(1)
    tm = h_ref.shape[0]
    nb = seg.nb
    halo = hp_ref.shape[0]
    steps = tm // nb
    t_abs = i * steps + lax.broadcasted_iota(jnp.int32, (tm, 1), 0) // nb

    for gi, win in enumerate(POOL_WINDOWS):
        @pl.when(g == gi)
        def _(win=win):
            left = win // 2
            right = win - 1 - left
            ext = _with_halo(h_ref, hp_ref, hn_ref, n_i)
            acc, width = ext, 1
            while width < win:
                shift = width * nb
                acc = acc[shift:] + acc[:acc.shape[0] - shift]
                width *= 2
            start = halo - left * nb
            total = acc[start:start + tm]
            cnt = (jnp.minimum(t_abs + right + 1, seg.t) - jnp.maximum(t_abs - left, 0)).astype(F32)
            h = h_ref[...]
            pooled = total / cnt - h
            y = jnp.dot(pooled.astype(BF16), w_ref[...].astype(BF16), preferred_element_type=F32) * sc_ref[...]
            o_ref[...] = x_ref[...] + _per_row(lambda a, b: a * b, y, gate_ref[...])


def _pool_mixer(seg, h, x, w_grp, scale, mod, l):
    ngrp, gsz, _ = w_grp.shape
    d = h.shape[1]
    tm = _tile_rows(seg, TM_MM)
    n_i = seg.m // tm
    halo = max(POOL_WINDOWS) // 2 * seg.nb
    hp, hn = _halo_specs(tm, n_i, halo, gsz, lambda j: j)
    return pl.pallas_call(
        functools.partial(_pool_kernel, seg, n_i),
        grid=(n_i, ngrp),
        in_specs=[_tile_spec(tm, gsz), hp, hn, _tile_spec(tm, gsz),
                  pl.BlockSpec((None, gsz, gsz), lambda i, j: (j, 0, 0)),
                  pl.BlockSpec((1, gsz), lambda i, j: (0, j)),
                  _mod_spec(seg, l, 2, gsz)],
        out_specs=_tile_spec(tm, gsz),
        out_shape=jax.ShapeDtypeStruct((seg.m, d), F32),
        compiler_params=_cparams(2),
        name="pool_mixer",
    )(h, h, h, x, w_grp, scale.reshape(1, d), mod)


def _shift_mix_kernel(nb, n_i, h_ref, hp_ref, hn_ref, mu_ref, *o_refs):
    tm = h_ref.shape[0]
    ext = _with_halo(h_ref, hp_ref, hn_ref, n_i)
    h = h_ref[...]
    xx = 0.5 * (ext[0:tm] + ext[2 * nb:2 * nb + tm]) - h
    for n, o_ref in enumerate(o_refs):
        o_ref[...] = (h + xx * mu_ref[n:n + 1, :]).astype(o_ref.dtype)


def _shift_mix(seg, h, mu):
    d = h.shape[1]
    tn = 512
    n_out = mu.shape[0]
    tm = _tile_rows(seg, TM_MM)
    n_i = seg.m // tm
    hp, hn = _halo_specs(tm, n_i, seg.nb, tn, lambda j: j)
    return pl.pallas_call(
        functools.partial(_shift_mix_kernel, seg.nb, n_i),
        grid=(n_i, d // tn),
        in_specs=[_tile_spec(tm, tn), hp, hn, pl.BlockSpec((n_out, tn), lambda i, j: (0, j))],
        out_specs=[_tile_spec(tm, tn)] * n_out,
        out_shape=[jax.ShapeDtypeStruct((seg.m, d), BF16)] * n_out,
        compiler_params=_cparams(2),
        name="rwkv_shift_mix",
    )(h, h, h, mu)


def _lanes_from_rows(rows):
    n_hp = rows.shape[1] // V7X_LANES
    stacked = jnp.concatenate([rows[:, hp * V7X_LANES:(hp + 1) * V7X_LANES] for hp in range(n_hp)], axis=0)
    return stacked.T


def _scan_kernel(nb, tb_steps, r_ref, k_ref, v_ref, w_ref, ag_ref, kk_ref, ka_ref, rk_ref, s0_ref,
                 o_ref, bonus_ref, s_ref, r_s, v_s, w_s, a_s, b_s, kd_s):
    e = pl.program_id(0)
    g = pl.program_id(1)
    tb = pl.program_id(2)
    n = RWKV_HEAD
    nvb = n // V7X_SUBLANES

    @pl.when(tb == 0)
    def _():
        s_ref[...] = s0_ref[...]

    def load(ref, t):
        row0 = pl.multiple_of(t * nb + g * V7X_SUBLANES, V7X_SUBLANES)
        return _lanes_from_rows(ref[pl.ds(row0, V7X_SUBLANES), :]).reshape(2, n, V7X_LANES)

    def prep(t, carry):
        k = load(k_ref, t)
        ag = load(ag_ref, t)
        r = load(r_ref, t)
        kk = k * kk_ref[...]
        kk = kk / jnp.maximum(jnp.sqrt(jnp.sum(kk * kk, axis=1, keepdims=True)), 1e-12)
        kd = k * (1.0 + (ag - 1.0) * ka_ref[...])
        r_s[t] = r
        v_s[t] = load(v_ref, t)
        w_s[t] = load(w_ref, t)
        a_s[t] = -kk
        b_s[t] = kk * ag
        kd_s[t] = kd
        bonus_ref[t] = jnp.sum(r * kd * rk_ref[...], axis=1)
        return carry

    lax.fori_loop(0, tb_steps, prep, 0)

    def vslice(vb):
        return pl.ds(vb * V7X_SUBLANES, V7X_SUBLANES)

    zeros = tuple(jnp.zeros((V7X_SUBLANES, V7X_LANES), F32) for _ in range(nvb))

    def step(s, carry):
        t = jnp.where(e == 0, s, tb_steps - 1 - s)
        for par in range(2):
            def pass1(kb, acc):
                acc = list(acc)
                for kr in range(V7X_SUBLANES):
                    ki = kb * V7X_SUBLANES + kr
                    a_row = a_s[t, par, pl.ds(ki, 1), :]
                    for vb in range(nvb):
                        acc[vb] = acc[vb] + s_ref[par, ki, vslice(vb), :] * a_row
                return tuple(acc)

            sa = lax.fori_loop(0, n // V7X_SUBLANES, pass1, zeros)
            vv = [v_s[t, par, vslice(vb), :] for vb in range(nvb)]

            def pass2(kb, acc):
                acc = list(acc)
                for kr in range(V7X_SUBLANES):
                    ki = kb * V7X_SUBLANES + kr
                    w_row = w_s[t, par, pl.ds(ki, 1), :]
                    b_row = b_s[t, par, pl.ds(ki, 1), :]
                    k_row = kd_s[t, par, pl.ds(ki, 1), :]
                    r_row = r_s[t, par, pl.ds(ki, 1), :]
                    for vb in range(nvb):
                        sn = s_ref[par, ki, vslice(vb), :] * w_row + sa[vb] * b_row + vv[vb] * k_row
                        s_ref[par, ki, vslice(vb), :] = sn
                        acc[vb] = acc[vb] + sn * r_row
                return tuple(acc)

            o = lax.fori_loop(0, n // V7X_SUBLANES, pass2, zeros)
            for vb in range(nvb):
                o_ref[t, par, vslice(vb), :] = o[vb]
        return carry

    lax.fori_loop(0, tb_steps, step, 0)


def _rwkv_scan(seg, r, k, v, w2, ag2, kk_p, ka_p, rk_p, s0):
    n, lanes = RWKV_HEAD, V7X_LANES
    d = r.shape[1]
    t = seg.t
    g = seg.nb // V7X_SUBLANES
    tb = min(SCAN_ROWS // seg.nb, t)
    nt = t // tb

    def tmap(e, tt):
        return jnp.where(e == 0, tt, nt - 1 - tt)

    shared = pl.BlockSpec((tb * seg.nb, d), lambda e, gg, tt: (tmap(e, tt), 0))
    per_dir = pl.BlockSpec((None, tb * seg.nb, d), lambda e, gg, tt: (e, tmap(e, tt), 0))
    par = pl.BlockSpec((2, n, lanes), lambda e, gg, tt: (0, 0, 0))
    st = pl.BlockSpec((None, None, 2, n, n, lanes), lambda e, gg, tt: (e, gg, 0, 0, 0, 0))
    o_spec = pl.BlockSpec((None, None, tb, 2, n, lanes), lambda e, gg, tt: (e, gg, tmap(e, tt), 0, 0, 0))
    bon_spec = pl.BlockSpec((None, None, tb, 2, lanes), lambda e, gg, tt: (e, gg, tmap(e, tt), 0, 0))
    scratch = pltpu.VMEM((tb, 2, n, lanes), F32)
    return pl.pallas_call(
        functools.partial(_scan_kernel, seg.nb, tb),
        grid=(2, g, nt),
        in_specs=[shared, shared, shared, per_dir, per_dir, par, par, par, st],
        out_specs=[o_spec, bon_spec, st],
        out_shape=[jax.ShapeDtypeStruct((2, g, t, 2, n, lanes), F32),
                   jax.ShapeDtypeStruct((2, g, t, 2, lanes), F32),
                   jax.ShapeDtypeStruct((2, g, 2, n, n, lanes), F32)],
        scratch_shapes=[scratch] * 6,
        compiler_params=_cparams(3),
        name="rwkv_scan",
    )(r, k, v, w2, ag2, kk_p, ka_p, rk_p, s0)


def _rwkv_post_kernel(nb, tb_steps, of_ref, ob_ref, v_ref, bf_ref, bb_ref, lnw_ref, lnb_ref, y_ref):
    g = pl.program_id(1)
    n = RWKV_HEAD
    n_hp = v_ref.shape[1] // V7X_LANES

    def body(t, carry):
        row0 = pl.multiple_of(t * nb + g * V7X_SUBLANES, V7X_SUBLANES)
        o = of_ref[t] + ob_ref[t]
        mean = jnp.mean(o, axis=1, keepdims=True)
        var = jnp.mean(jnp.square(o - mean), axis=1, keepdims=True)
        on = (o - mean) * lax.rsqrt(var + GN_EPS)
        v = _lanes_from_rows(v_ref[pl.ds(row0, V7X_SUBLANES), :]).reshape(2, n, V7X_LANES)
        bonus = (bf_ref[t] + bb_ref[t])[:, None, :]
        y = on * lnw_ref[...] + lnb_ref[...] + bonus * v
        rows = y.reshape(2 * n, V7X_LANES).T
        for hp in range(n_hp):
            y_ref[pl.ds(row0, V7X_SUBLANES), hp * V7X_LANES:(hp + 1) * V7X_LANES] = (
                rows[hp * V7X_SUBLANES:(hp + 1) * V7X_SUBLANES, :])
        return carry

    lax.fori_loop(0, tb_steps, body, 0)


def _rwkv_post(seg, o2, v, bonus2, lnw_p, lnb_p):
    _, g, t, _, n, lanes = o2.shape
    d = v.shape[1]
    tb = min(SCAN_ROWS // seg.nb, t)
    blk = lambda e: pl.BlockSpec((None, None, tb, 2, n, lanes), lambda tt, gg: (e, gg, tt, 0, 0, 0))
    bon = lambda e: pl.BlockSpec((None, None, tb, 2, lanes), lambda tt, gg: (e, gg, tt, 0, 0))
    par = pl.BlockSpec((2, n, lanes), lambda tt, gg: (0, 0, 0))
    nat = pl.BlockSpec((tb * seg.nb, d), lambda tt, gg: (tt, 0))
    return pl.pallas_call(
        functools.partial(_rwkv_post_kernel, seg.nb, tb),
        grid=(t // tb, g),
        in_specs=[blk(0), blk(1), nat, bon(0), bon(1), par, par],
        out_specs=nat,
        out_shape=jax.ShapeDtypeStruct((seg.m, d), F32),
        compiler_params=_cparams(2),
        name="rwkv_post",
    )(o2, o2, v, bonus2, bonus2, lnw_p, lnb_p)


def _param_lanes(p):
    h = p.size // RWKV_HEAD
    q = p.reshape(h // 2, 2, RWKV_HEAD).transpose(1, 2, 0)
    return jnp.repeat(q, V7X_LANES // (h // 2), axis=2)


def _pad_lora_in(w):
    r = w.shape[-1]
    w = jnp.pad(w, ((0, 0), (0, 0), (0, LORA_PAD - r)))
    return jnp.concatenate([w[0], w[1]], axis=1)


def _pad_lora_out(w):
    r = w.shape[1]
    z = jnp.zeros((LORA_PAD, w.shape[2]), w.dtype)
    wp = jnp.pad(w, ((0, 0), (0, LORA_PAD - r), (0, 0)))
    return jnp.stack([jnp.concatenate([wp[0], z], axis=0), jnp.concatenate([z, wp[1]], axis=0)])


def _dir_mm(seg, x, w2, bias2, epi, name):
    k = x.shape[1]
    d = w2.shape[2]
    tm = _tile_rows(seg, TM_MM)
    n_i = seg.m // tm
    tn = 512
    return pl.pallas_call(
        functools.partial(_mm_kernel, 0, 1, None, epi),
        grid=(2 * n_i, d // tn),
        in_specs=[pl.BlockSpec((tm, k), lambda i, j: (i % n_i, 0)),
                  pl.BlockSpec((None, k, tn), lambda i, j: (i // n_i, 0, j)),
                  pl.BlockSpec((None, 1, tn), lambda i, j: (i // n_i, 0, j))],
        out_specs=pl.BlockSpec((None, tm, tn), lambda i, j: (i // n_i, i % n_i, j)),
        out_shape=jax.ShapeDtypeStruct((2, seg.m, d), F32),
        compiler_params=_cparams(2),
        name=name,
    )(x, w2, bias2.reshape(2, 1, d))


def _rwkv_mixer(seg, h, x, p, state, mod, l):
    d = h.shape[1]
    xr, xw, xk, xv, xa, xg = _shift_mix(seg, h, p['mu'])
    mmf = functools.partial(_mm, seg, tn=512)
    r = mmf(xr, p['w_r'], _w2_spec(d, 512), d, out_dtype=F32, name="rwkv_r")
    k = mmf(xk, p['w_k'], _w2_spec(d, 512), d, out_dtype=F32, name="rwkv_k")
    v = mmf(xv, p['w_v'], _w2_spec(d, 512), d, out_dtype=F32, name="rwkv_v")
    gl = _mm(seg, xg, p['g1'], _w2_spec(d, 256), 256, tn=256, out_dtype=BF16, name="rwkv_g1", epi=_epi_sigmoid)
    gate = mmf(gl, p['g2'], _w2_spec(256, 512), d, out_dtype=F32, name="rwkv_g2")
    nl = 2 * LORA_PAD
    wl = _mm(seg, xw, p['w1p'], _w2_spec(d, nl), nl, tn=nl, out_dtype=BF16, name="rwkv_w1", epi=_epi_tanh)
    al = _mm(seg, xa, p['a1p'], _w2_spec(d, nl), nl, tn=nl, out_dtype=BF16, name="rwkv_a1")
    dec2 = _dir_mm(seg, wl, p['w2p'], p['w0'], _epi_decay, "rwkv_w2")
    ag2 = _dir_mm(seg, al, p['a2p'], p['a0'], _epi_bias_sigmoid, "rwkv_a2")

    n_heads = d // RWKV_HEAD
    n = RWKV_HEAD
    g = seg.nb // V7X_SUBLANES
    if state is None:
        s0 = jnp.zeros((2, g, 2, n, n, V7X_LANES), F32)
    else:
        s0 = state.reshape(g, V7X_SUBLANES, 2, n_heads // 2, 2, n, n).transpose(2, 0, 4, 6, 5, 3, 1)
        s0 = s0.reshape(2, g, 2, n, n, V7X_LANES)
    o2, bonus2, sfin = _rwkv_scan(seg, r, k, v, dec2, ag2, p['kk_p'], p['ka_p'], p['rk_p'], s0)
    y = _rwkv_post(seg, o2, v, bonus2, p['lnw_p'], p['lnb_p'])
    x_new = _mm(seg, y, p['w_o'], _w2_spec(d, 512), d, tn=512, tm=TM_ROW, out_dtype=F32, name="rwkv_wo",
                pro=_pro_mul, pro_args=(('full', gate),),
                epi=_epi_resid, epi_args=(('tile', x), ('mod', mod, l, 2)))
    return x_new, sfin


def _head_rmsnorm(acc, w_ref):
    outs = []
    for hh in range(acc.shape[1] // ATTN_HEAD_DIM):
        xh = acc[:, hh * ATTN_HEAD_DIM:(hh + 1) * ATTN_HEAD_DIM]
        yh = xh * lax.rsqrt(jnp.mean(xh * xh, axis=-1, keepdims=True) + NORM_EPS)
        outs.append(yh * w_ref[...])
    return jnp.concatenate(outs, axis=1) if len(outs) > 1 else outs[0]


def _rope(y, cos_ref, sin_ref):
    nh = y.shape[1] // ATTN_HEAD_DIM
    cos = jnp.concatenate([cos_ref[...]] * nh, axis=1) if nh > 1 else cos_ref[...]
    sin = jnp.concatenate([sin_ref[...]] * nh, axis=1) if nh > 1 else sin_ref[...]
    width = y.shape[1]
    lane = lax.broadcasted_iota(jnp.int32, (1, width), 1)
    quarter = ATTN_HEAD_DIM // 4
    first = (lane & (2 * quarter - 1)) < quarter
    swapped = jnp.where(first, pltpu.roll(y, width - quarter, axis=1), pltpu.roll(y, quarter, axis=1))
    return y * cos + swapped * sin


def _epi_qk_rope(acc, nw_ref, cos_ref, sin_ref):
    return _rope(_head_rmsnorm(acc, nw_ref), cos_ref, sin_ref)


def _epi_qk(acc, nw_ref):
    return _head_rmsnorm(acc, nw_ref)


def _rope_tables(seg):
    t = seg.t
    row = (jnp.arange(t) // GRID_W).astype(F32)
    col = (jnp.arange(t) % GRID_W).astype(F32)
    n_freq = ATTN_HEAD_DIM // 4
    inv = ROPE_BASE ** (-jnp.arange(n_freq, dtype=F32) / n_freq)
    ar, ac = row[:, None] * inv, col[:, None] * inv
    cos = jnp.concatenate([jnp.cos(ar), jnp.cos(ar), jnp.cos(ac), jnp.cos(ac)], axis=1)
    sin = jnp.concatenate([-jnp.sin(ar), jnp.sin(ar), -jnp.sin(ac), jnp.sin(ac)], axis=1)
    return jnp.repeat(cos, seg.nb, axis=0), jnp.repeat(sin, seg.nb, axis=0)


def _attn_kernel(seq_nb, n_band, nblk, has_cache, *refs):
    n_tm = n_band if n_band else 1
    n_src = n_tm + (1 if has_cache else 0)
    q_refs = refs[:ATTN_GROUP]
    refs = refs[ATTN_GROUP:]
    k_refs = refs[:n_src]
    v_refs = refs[n_src:2 * n_src]
    sink_ref = refs[2 * n_src]
    o_refs = refs[2 * n_src + 1:]
    blk = pl.program_id(1)
    bq = q_refs[0].shape[0] // seq_nb

    def src(refs_, si, b):
        if si < n_tm:
            rows = refs_[si].shape[0] // seq_nb
            return refs_[si][pl.ds(b, rows, stride=seq_nb), :]
        return refs_[si][b]

    for b in range(seq_nb):
        q = jnp.concatenate([q_ref[pl.ds(b, bq, stride=seq_nb), :] for q_ref in q_refs], axis=0).astype(BF16)
        scores = []
        for si in range(n_src):
            s = lax.dot_general(q, src(k_refs, si, b).astype(BF16), (((1,), (1,)), ((), ())),
                                preferred_element_type=F32) * ATTN_SCALE
            if n_band == 3 and si in (0, 2):
                qi = lax.broadcasted_iota(jnp.int32, (ATTN_GROUP * bq, bq), 0) & (bq - 1)
                ki = lax.broadcasted_iota(jnp.int32, (ATTN_GROUP * bq, bq), 1)
                if si == 0:
                    valid, present = ki >= qi, blk > 0
                else:
                    valid, present = ki <= qi, blk < nblk - 1
                s = jnp.where(valid, s, NEG_INF) + jnp.where(present, 0.0, NEG_INF)
            scores.append(s)
        sink = sink_ref[...]
        m = sink
        for s in scores:
            m = jnp.maximum(m, jnp.max(s, axis=-1, keepdims=True))
        ps = [jnp.exp(s - m) for s in scores]
        den = jnp.exp(sink - m)
        for p in ps:
            den = den + jnp.sum(p, axis=-1, keepdims=True)
        inv = 1.0 / den
        out = None
        for si in range(n_src):
            o = jnp.dot((ps[si] * inv).astype(BF16), src(v_refs, si, b).astype(BF16), preferred_element_type=F32)
            out = o if out is None else out + o
        for gq, o_ref in enumerate(o_refs):
            o_ref[pl.ds(b, bq, stride=seq_nb), :] = out[gq * bq:(gq + 1) * bq, :]


def _pro_heads(x0, *other_refs):
    parts = [x0] + [r[...] for r in other_refs]
    return jnp.concatenate([parts[gq][kv] for kv in range(x0.shape[0]) for gq in range(len(parts))], axis=1)


def _attn_mixer(seg, h, x, p, cache_k, cache_v, mod, l):
    d = h.shape[1]
    dh = ATTN_HEAD_DIM
    nkv = p['w_qkv'].shape[1] // dh // (ATTN_GROUP + 2)
    nq = nkv * ATTN_GROUP
    tn = 512
    norm_arg = lambda w: ('raw', w.reshape(1, dh), pl.BlockSpec((1, dh), lambda i, j: (0, 0)))
    if cache_k is None:
        epi, extra = _epi_qk, ()
    else:
        cos, sin = _rope_tables(seg)
        tm_q = _tile_rows(seg, TM_MM)
        tab = lambda a: ('raw', a, pl.BlockSpec((tm_q, dh), lambda i, j: (i, 0)))
        epi, extra = _epi_qk_rope, (tab(cos), tab(sin))
    q = _mm(seg, h, p['w_qkv'], _w2_spec(d, tn), nq * dh, tn=tn, out_dtype=F32, name="attn_q", epi=epi,
            epi_args=(norm_arg(p['q_norm']),) + extra, head_dim_out=dh)
    k = _mm(seg, h, p['w_qkv'], _w2_spec(d, tn, nq * dh // tn), nkv * dh, tn=tn, out_dtype=F32, name="attn_k",
            epi=epi, epi_args=(norm_arg(p['k_norm']),) + extra)
    v = _mm(seg, h, p['w_qkv'], _w2_spec(d, tn, (nq + nkv) * dh // tn), nkv * dh, tn=tn, out_dtype=F32,
            name="attn_v")

    bq = ATTN_BLOCK
    nblk = seg.t // bq
    rows = bq * seg.nb
    sink_col = jnp.repeat(p['sink'].reshape(nkv, ATTN_GROUP), bq, axis=1).reshape(nkv, ATTN_GROUP * bq, 1)
    sink_spec = pl.BlockSpec((None, ATTN_GROUP * bq, 1), lambda kv, qb: (kv, 0, 0))
    q_specs = [pl.BlockSpec((None, rows, dh), lambda kv, qb, gq=gq: (kv * ATTN_GROUP + gq, qb, 0))
               for gq in range(ATTN_GROUP)]
    q_args = (q,) * ATTN_GROUP
    if cache_k is None:
        full = pl.BlockSpec((seg.m, dh), lambda kv, qb: (0, kv))
        in_specs = [*q_specs, full, full, sink_spec]
        args = (*q_args, k, v, sink_col)
        n_band = 0
    else:
        past = cache_k.shape[1]
        ck = cache_k.reshape(seg.nb, past, nkv * dh)
        cv = cache_v.reshape(seg.nb, past, nkv * dh)

        def band(delta):
            return pl.BlockSpec((rows, dh), lambda kv, qb: (jnp.clip(qb + delta, 0, nblk - 1), kv))

        cache_spec = pl.BlockSpec((seg.nb, past, dh), lambda kv, qb: (0, 0, kv))
        in_specs = [*q_specs, band(-1), band(0), band(1), cache_spec, band(-1), band(0), band(1), cache_spec,
                    sink_spec]
        args = (*q_args, k, k, k, ck, v, v, v, cv, sink_col)
        n_band = 3
    o_parts = pl.pallas_call(
        functools.partial(_attn_kernel, seg.nb, n_band, nblk, cache_k is not None),
        grid=(nkv, nblk),
        in_specs=in_specs,
        out_specs=[pl.BlockSpec((None, rows, dh), lambda kv, qb: (kv, qb, 0))] * ATTN_GROUP,
        out_shape=[jax.ShapeDtypeStruct((nkv, seg.m, dh), F32)] * ATTN_GROUP,
        compiler_params=_cparams(2),
        name="attn_core",
    )(*args)

    tm_o = _tile_rows(seg, TM_ROW)
    o_spec = pl.BlockSpec((nkv, tm_o, dh), lambda i, j: (0, i, 0))
    x_new = _mm(seg, o_parts[0], p['w_o'], _w2_spec(d, 512), d, tn=512, tm=TM_ROW, out_dtype=F32, name="attn_wo",
                k=nq * dh, x_spec=o_spec, pro=_pro_heads, pro_args=tuple(('raw', a, o_spec) for a in o_parts[1:]),
                epi=_epi_resid, epi_args=(('tile', x), ('mod', mod, l, 2)))
    return x_new, k, v


def _ffn_up_kernel(nb, n_i, x_ref, xp_ref, xn_ref, wg_ref, wv_ref, cwg_ref, cwv_ref, cbg_ref, cbv_ref, o_ref):
    tm = x_ref.shape[0]
    halo = xp_ref.shape[0]
    xh = _with_halo(x_ref, xp_ref, xn_ref, n_i)

    def conv(w_ref, cw_ref, cb_ref):
        u = jnp.dot(xh, w_ref[...].astype(BF16), preferred_element_type=F32)
        prev = u[halo - nb:halo - nb + tm]
        cur = u[halo:halo + tm]
        nxt = u[halo + nb:halo + nb + tm]
        return prev * cw_ref[0:1, :] + cur * cw_ref[1:2, :] + nxt * cw_ref[2:3, :] + cb_ref[...]

    gate = conv(wg_ref, cwg_ref, cbg_ref)
    val = conv(wv_ref, cwv_ref, cbv_ref)
    o_ref[...] = (gate * jax.nn.sigmoid(gate) * val).astype(o_ref.dtype)


def _conv_ffn(seg, h, x, up, conv_w, conv_b, down, mod, l):
    d = h.shape[1]
    f = down.shape[1]
    tn = 256
    nj = f // tn
    tm = _tile_rows(seg, TM_FFN)
    n_i = seg.m // tm
    halo = max(V7X_BF16_ROWS, seg.nb)
    hp, hn = _halo_specs(tm, n_i, halo, d, lambda j: 0)
    cw = lambda off: pl.BlockSpec((None, conv_w.shape[1], tn), lambda i, j: (l, 0, j + off))
    cb = lambda off: pl.BlockSpec((None, 1, tn), lambda i, j: (l, 0, j + off))
    cb3 = conv_b.reshape(conv_b.shape[0], 1, 2 * f)
    act = pl.pallas_call(
        functools.partial(_ffn_up_kernel, seg.nb, n_i),
        grid=(n_i, nj),
        in_specs=[_full_spec(tm, d), hp, hn, _w3_spec(l, d, tn), _w3_spec(l, d, tn, nj), cw(0), cw(nj), cb(0), cb(nj)],
        out_specs=_tile_spec(tm, tn),
        out_shape=jax.ShapeDtypeStruct((seg.m, f), BF16),
        compiler_params=_cparams(2),
        name="ffn_up",
    )(h, h, h, up, up, conv_w, conv_w, cb3, cb3)
    return _mm(seg, act, down, _w3_spec(l, f, 256), d, tn=256, out_dtype=F32, name="ffn_down",
               epi=_epi_resid, epi_args=(('tile', x), ('mod', mod, l, 5)))


def kernel(x_prompt, x_sample, state_rwkv, cache_k, cache_v, c, c_ctx, ada_w, ada_b, norm_mix, norm_ffn, ffn_up, ffn_conv_w, ffn_conv_b, ffn_down, pool_w, pool_scale, rwkv_mu, rwkv_w_r, rwkv_w_k, rwkv_w_v, rwkv_w0, rwkv_w1, rwkv_w2, rwkv_a0, rwkv_a1, rwkv_a2, rwkv_g1, rwkv_g2, rwkv_k_k, rwkv_k_a, rwkv_r_k, rwkv_ln_w, rwkv_ln_b, rwkv_w_o, attn_w_qkv, attn_q_norm, attn_k_norm, attn_sink, attn_w_o):
    n_ctx, l_ctx, d = x_prompt.shape
    n_lat, l_lat, _ = x_sample.shape
    depth = ada_w.shape[0]
    assert n_ctx % V7X_SUBLANES == 0 and n_lat == V7X_SUBLANES, "8 latent sequences share one conditioning tile"
    assert d // RWKV_HEAD // 2 * V7X_SUBLANES == V7X_LANES, "8 sequences x head pairs must fill the lanes"

    n_cond = n_lat + 1
    ncp = -(-n_cond // V7X_SUBLANES) * V7X_SUBLANES
    cond = jnp.concatenate([c, c_ctx[None, :], jnp.zeros((ncp - n_cond, d), F32)], axis=0)
    mod = _adaln(cond, ada_w, ada_b)
    mod4 = mod.reshape(depth, ncp, 6, d).transpose(0, 2, 1, 3)
    mods = (mod4[:, :, n_lat:n_lat + 1], mod4[:, :, :n_lat])

    segs = (_Seg(n_ctx, l_ctx, 1), _Seg(n_lat, l_lat, n_lat))
    xs = [a.transpose(1, 0, 2).reshape(a.shape[0] * a.shape[1], d) for a in (x_prompt, x_sample)]

    n_mixers = 3
    slots = [0] * n_mixers
    new_state, new_k, new_v = [], [], []
    for l in range(depth):
        kind = l % n_mixers
        slot = slots[kind]
        slots[kind] += 1
        last = l == depth - 1
        if kind == 1:
            p = {'mu': rwkv_mu[slot], 'w_r': rwkv_w_r[slot], 'w_k': rwkv_w_k[slot], 'w_v': rwkv_w_v[slot],
                 'w0': rwkv_w0[slot], 'w1p': _pad_lora_in(rwkv_w1[slot]), 'w2p': _pad_lora_out(rwkv_w2[slot]),
                 'a0': rwkv_a0[slot], 'a1p': _pad_lora_in(rwkv_a1[slot]), 'a2p': _pad_lora_out(rwkv_a2[slot]),
                 'g1': rwkv_g1[slot], 'g2': rwkv_g2[slot], 'w_o': rwkv_w_o[slot],
                 'kk_p': _param_lanes(rwkv_k_k[slot]), 'ka_p': _param_lanes(rwkv_k_a[slot]),
                 'rk_p': _param_lanes(rwkv_r_k[slot]), 'lnw_p': _param_lanes(rwkv_ln_w[slot]),
                 'lnb_p': _param_lanes(rwkv_ln_b[slot])}
        elif kind == 2:
            p = {'w_qkv': attn_w_qkv[slot], 'q_norm': attn_q_norm[slot], 'k_norm': attn_k_norm[slot],
                 'sink': attn_sink[slot], 'w_o': attn_w_o[slot]}
        for si, seg in enumerate(segs):
            mod_s = mods[si]
            h_dtype = BF16 if kind == 2 else F32
            x = xs[si]
            h = _norm_mod(seg, x, norm_mix, mod_s, l, 0, 1, h_dtype)
            if kind == 0:
                x = _pool_mixer(seg, h, x, pool_w[slot], pool_scale[slot], mod_s, l)
            elif kind == 1:
                x, sfin = _rwkv_mixer(seg, h, x, p, state_rwkv[:, slot] if si == 1 else None, mod_s, l)
                if si == 0:
                    g = seg.nb // V7X_SUBLANES
                    hp = d // RWKV_HEAD // 2
                    st = sfin.reshape(2, g, 2, RWKV_HEAD, RWKV_HEAD, hp, V7X_SUBLANES).transpose(1, 6, 0, 5, 2, 4, 3)
                    new_state.append(st.reshape(seg.nb, 2, 2 * hp, RWKV_HEAD, RWKV_HEAD))
            else:
                x, k, v = _attn_mixer(seg, h, x, p, cache_k[:, slot] if si == 1 else None,
                                      cache_v[:, slot] if si == 1 else None, mod_s, l)
                if si == 0:
                    nkv = k.shape[1] // ATTN_HEAD_DIM
                    to_bm = lambda a: a.reshape(seg.t, seg.nb, nkv, ATTN_HEAD_DIM).transpose(1, 0, 2, 3)
                    new_k.append(to_bm(k))
                    new_v.append(to_bm(v))
            h2 = _norm_mod(seg, x, norm_ffn, mod_s, l, 3, 4, BF16)
            xs[si] = _conv_ffn(seg, h2, x, ffn_up, ffn_conv_w, ffn_conv_b, ffn_down, mod_s, l)

    ys = [a.reshape(seg.t, seg.nb, d).transpose(1, 0, 2) for a, seg in zip(xs, segs)]
    return (ys[0], ys[1], jnp.stack(new_state, axis=1), jnp.stack(new_k, axis=1), jnp.stack(new_v, axis=1))
```

```python
import functools
from typing import NamedTuple

import jax
import jax.numpy as jnp
from jax import lax
from jax.experimental import pallas as pl
from jax.experimental.pallas import tpu as pltpu

F32 = jnp.float32
BF16 = jnp.bfloat16

V7X_LANES = 128
V7X_SUBLANES = 8
V7X_BF16_ROWS = 16
V7X_VMEM_LIMIT_BYTES = 56 * 1024 * 1024

NORM_EPS = 1e-6
GN_EPS = 64e-5
DECAY_SCALE = 0.606531
POOL_WINDOWS = (2, 4, 8, 16)
RWKV_HEAD = 64
ATTN_HEAD_DIM = 128
ATTN_GROUP = 4
ATTN_BLOCK = 128
GRID_W = 64
ROPE_BASE = 10000.0
ATTN_SCALE = ATTN_HEAD_DIM ** -0.5
NEG_INF = -1e30
LORA_PAD = 128

TM_FFN = 2048
TM_MM = 1024
TM_ROW = 512
SCAN_ROWS = 128


class _Seg(NamedTuple):
    nb: int
    t: int
    mod_rows: int

    @property
    def m(self):
        return self.nb * self.t


def _cparams(n_axes):
    return pltpu.CompilerParams(dimension_semantics=("arbitrary",) * n_axes,
                                vmem_limit_bytes=V7X_VMEM_LIMIT_BYTES)


def _tile_rows(seg, target):
    return min(target, seg.m)


def _full_spec(tm, w):
    return pl.BlockSpec((tm, w), lambda i, j: (i, 0))


def _tile_spec(tm, tn):
    return pl.BlockSpec((tm, tn), lambda i, j: (i, j))


def _mod_spec(seg, l, which, tn):
    return pl.BlockSpec((None, None, seg.mod_rows, tn), lambda i, j: (l, which, 0, j))


def _per_row(fn, y, m):
    if m.shape[0] == 1:
        return fn(y, m)
    tm, w = y.shape
    return fn(y.reshape(tm // m.shape[0], m.shape[0], w), m[None]).reshape(tm, w)


def _halo_specs(tm, n_i, rows, width_block, col_fn):
    per = tm // rows
    n_blocks = n_i * per
    prev = pl.BlockSpec((rows, width_block), lambda i, j: (jnp.maximum(i * per - 1, 0), col_fn(j)))
    nxt = pl.BlockSpec((rows, width_block), lambda i, j: (jnp.minimum((i + 1) * per, n_blocks - 1), col_fn(j)))
    return prev, nxt


def _with_halo(x_ref, prev_ref, next_ref, n_i):
    i = pl.program_id(0)
    prev = jnp.where(i > 0, prev_ref[...], jnp.zeros_like(prev_ref[...]))
    nxt = jnp.where(i < n_i - 1, next_ref[...], jnp.zeros_like(next_ref[...]))
    return jnp.concatenate([prev, x_ref[...], nxt], axis=0)


def _adaln_kernel(c_ref, w_ref, b_ref, o_ref):
    c = c_ref[...]
    s = c * jax.nn.sigmoid(c)
    acc = jnp.dot(s.astype(BF16), w_ref[...].astype(BF16), preferred_element_type=F32)
    o_ref[...] = acc + b_ref[...]


def _adaln(cond, ada_w, ada_b):
    depth, d, n6 = ada_w.shape
    ncp = cond.shape[0]
    tn = 1024
    return pl.pallas_call(
        _adaln_kernel,
        grid=(depth, n6 // tn),
        in_specs=[pl.BlockSpec((ncp, d), lambda l, j: (0, 0)),
                  pl.BlockSpec((None, d, tn), lambda l, j: (l, 0, j)),
                  pl.BlockSpec((None, 1, tn), lambda l, j: (l, 0, j))],
        out_specs=pl.BlockSpec((None, ncp, tn), lambda l, j: (l, 0, j)),
        out_shape=jax.ShapeDtypeStruct((depth, ncp, n6), F32),
        compiler_params=_cparams(2),
        name="adaln",
    )(cond, ada_w, ada_b.reshape(depth, 1, n6))


def _norm_mod_math(x, g, sh, sc):
    y = x * lax.rsqrt(jnp.mean(x * x, axis=-1, keepdims=True) + NORM_EPS)
    y = y * g
    return _per_row(lambda a, b: a * (1.0 + b), y, sc), sh


def _norm_mod_kernel(x_ref, g_ref, sh_ref, sc_ref, o_ref):
    y, sh = _norm_mod_math(x_ref[...], g_ref[...], sh_ref[...], sc_ref[...])
    o_ref[...] = _per_row(lambda a, b: a + b, y, sh).astype(o_ref.dtype)


def _norm_mod(seg, x, g, mod, l, shift_i, scale_i, out_dtype):
    d = g.shape[-1]
    tm = _tile_rows(seg, TM_ROW)
    return pl.pallas_call(
        _norm_mod_kernel,
        grid=(seg.m // tm, 1),
        in_specs=[_full_spec(tm, d), pl.BlockSpec((None, 1, d), lambda i, j: (l, 0, 0)),
                  _mod_spec(seg, l, shift_i, d), _mod_spec(seg, l, scale_i, d)],
        out_specs=_full_spec(tm, d),
        out_shape=jax.ShapeDtypeStruct((seg.m, d), out_dtype),
        compiler_params=_cparams(2),
        name="norm_mod",
    )(x, g.reshape(g.shape[0], 1, d), mod, mod)


def _mm_kernel(n_pro, n_epi, pro, epi, x_ref, *refs):
    pro_refs = refs[:n_pro]
    w_ref = refs[n_pro]
    epi_refs = refs[n_pro + 1:n_pro + 1 + n_epi]
    o_ref = refs[n_pro + 1 + n_epi]
    x = x_ref[...]
    if pro is not None:
        x = pro(x, *pro_refs)
    acc = jnp.dot(x.astype(BF16), w_ref[...].astype(BF16), preferred_element_type=F32)
    if epi is not None:
        acc = epi(acc, *epi_refs)
    if len(o_ref.shape) == 3:
        hd = o_ref.shape[2]
        for hh in range(o_ref.shape[0]):
            o_ref[hh] = acc[:, hh * hd:(hh + 1) * hd].astype(o_ref.dtype)
    else:
        o_ref[...] = acc.astype(o_ref.dtype)


def _mm(seg, x, w, w_spec, n, *, tn, out_dtype, name, tm=None, k=None, x_spec=None, pro=None, pro_args=(),
        epi=None, epi_args=(), head_dim_out=None):
    k = x.shape[1] if k is None else k
    tm = _tile_rows(seg, TM_MM if tm is None else tm)

    def lower(arg):
        kind = arg[0]
        if kind == 'full':
            return arg[1], _full_spec(tm, arg[1].shape[1])
        if kind == 'tile':
            return arg[1], _tile_spec(tm, tn)
        if kind == 'row':
            return arg[1], pl.BlockSpec((1, tn), lambda i, j: (0, j))
        if kind == 'mod':
            return arg[1], _mod_spec(seg, arg[2], arg[3], tn)
        return arg[1], arg[2]

    pro_l = [lower(a) for a in pro_args]
    epi_l = [lower(a) for a in epi_args]
    if head_dim_out is None:
        out_spec = _tile_spec(tm, tn)
        out_shape = jax.ShapeDtypeStruct((seg.m, n), out_dtype)
    else:
        out_spec = pl.BlockSpec((tn // head_dim_out, tm, head_dim_out), lambda i, j: (j, i, 0))
        out_shape = jax.ShapeDtypeStruct((n // head_dim_out, seg.m, head_dim_out), out_dtype)
    return pl.pallas_call(
        functools.partial(_mm_kernel, len(pro_l), len(epi_l), pro, epi),
        grid=(seg.m // tm, n // tn),
        in_specs=[_full_spec(tm, k) if x_spec is None else x_spec, *[sp for _, sp in pro_l], w_spec,
                  *[sp for _, sp in epi_l]],
        out_specs=out_spec,
        out_shape=out_shape,
        compiler_params=_cparams(2),
        name=name,
    )(x, *[a for a, _ in pro_l], w, *[a for a, _ in epi_l])


def _w2_spec(k, tn, joff=0):
    return pl.BlockSpec((k, tn), lambda i, j: (0, j + joff))


def _w3_spec(l, k, tn, joff=0):
    return pl.BlockSpec((None, k, tn), lambda i, j: (l, 0, j + joff))


def _epi_resid(acc, x_ref, gate_ref):
    return x_ref[...] + _per_row(lambda a, b: a * b, acc, gate_ref[...])


def _epi_sigmoid(acc):
    return jax.nn.sigmoid(acc)


def _epi_tanh(acc):
    return jnp.tanh(acc)


def _epi_decay(acc, w0_ref):
    return jnp.exp(-DECAY_SCALE * jax.nn.sigmoid(w0_ref[...] + acc))


def _epi_bias_sigmoid(acc, a0_ref):
    return jax.nn.sigmoid(a0_ref[...] + acc)


def _pro_mul(x, g_ref):
    return x * g_ref[...]


def _pool_kernel(seg, n_i, h_ref, hp_ref, hn_ref, x_ref, w_ref, sc_ref, gate_ref, o_ref):
    i = pl.program_id(0)
    g = pl.program_id(1)
    tm = h_ref.shape[0]
    nb = seg.nb
    halo = hp_ref.shape[0]
    steps = tm // nb
    t_abs = i * steps + lax.broadcasted_iota(jnp.int32, (tm, 1), 0) // nb

    for gi, win in enumerate(POOL_WINDOWS):
        @pl.when(g == gi)
        def _(win=win):
            left = win // 2
            right = win - 1 - left
            ext = _with_halo(h_ref, hp_ref, hn_ref, n_i)
            acc, width = ext, 1
            while width < win:
                shift = width * nb
                acc = acc[shift:] + acc[:acc.shape[0] - shift]
                width *= 2
            start = halo - left * nb
            total = acc[start:start + tm]
            cnt = (jnp.minimum(t_abs + right + 1, seg.t) - jnp.maximum(t_abs - left, 0)).astype(F32)
            h = h_ref[...]
            pooled = total / cnt - h
            y = jnp.dot(pooled.astype(BF16), w_ref[...].astype(BF16), preferred_element_type=F32) * sc_ref[...]
            o_ref[...] = x_ref[...] + _per_row(lambda a, b: a * b, y, gate_ref[...])


def _pool_mixer(seg, h, x, w_grp, scale, mod, l):
    ngrp, gsz, _ = w_grp.shape
    d = h.shape[1]
    tm = _tile_rows(seg, TM_MM)
    n_i = seg.m // tm
    halo = max(POOL_WINDOWS) // 2 * seg.nb
    hp, hn = _halo_specs(tm, n_i, halo, gsz, lambda j: j)
    return pl.pallas_call(
        functools.partial(_pool_kernel, seg, n_i),
        grid=(n_i, ngrp),
        in_specs=[_tile_spec(tm, gsz), hp, hn, _tile_spec(tm, gsz),
                  pl.BlockSpec((None, gsz, gsz), lambda i, j: (j, 0, 0)),
                  pl.BlockSpec((1, gsz), lambda i, j: (0, j)),
                  _mod_spec(seg, l, 2, gsz)],
        out_specs=_tile_spec(tm, gsz),
        out_shape=jax.ShapeDtypeStruct((seg.m, d), F32),
        compiler_params=_cparams(2),
        name="pool_mixer",
    )(h, h, h, x, w_grp, scale.reshape(1, d), mod)


def _shift_mix_kernel(nb, n_i, h_ref, hp_ref, hn_ref, mu_ref, *o_refs):
    tm = h_ref.shape[0]
    ext = _with_halo(h_ref, hp_ref, hn_ref, n_i)
    h = h_ref[...]
    xx = 0.5 * (ext[0:tm] + ext[2 * nb:2 * nb + tm]) - h
    for n, o_ref in enumerate(o_refs):
        o_ref[...] = (h + xx * mu_ref[n:n + 1, :]).astype(o_ref.dtype)


def _shift_mix(seg, h, mu):
    d = h.shape[1]
    tn = 512
    n_out = mu.shape[0]
    tm = _tile_rows(seg, TM_MM)
    n_i = seg.m // tm
    hp, hn = _halo_specs(tm, n_i, seg.nb, tn, lambda j: j)
    return pl.pallas_call(
        functools.partial(_shift_mix_kernel, seg.nb, n_i),
        grid=(n_i, d // tn),
        in_specs=[_tile_spec(tm, tn), hp, hn, pl.BlockSpec((n_out, tn), lambda i, j: (0, j))],
        out_specs=[_tile_spec(tm, tn)] * n_out,
        out_shape=[jax.ShapeDtypeStruct((seg.m, d), BF16)] * n_out,
        compiler_params=_cparams(2),
        name="rwkv_shift_mix",
    )(h, h, h, mu)


def _lanes_from_rows(rows):
    n_hp = rows.shape[1] // V7X_LANES
    stacked = jnp.concatenate([rows[:, hp * V7X_LANES:(hp + 1) * V7X_LANES] for hp in range(n_hp)], axis=0)
    return stacked.T


def _scan_kernel(nb, tb_steps, r_ref, k_ref, v_ref, w_ref, ag_ref, kk_ref, ka_ref, rk_ref, s0_ref,
                 o_ref, bonus_ref, s_ref, r_s, v_s, w_s, a_s, b_s, kd_s, p_last):
    e = pl.program_id(0)
    g = pl.program_id(1)
    tb = pl.program_id(2)
    n = RWKV_HEAD
    nvb = n // V7X_SUBLANES

    @pl.when(tb == 0)
    def _():
        s_ref[...] = s0_ref[...]

    def load(ref, t):
        row0 = pl.multiple_of(t * nb + g * V7X_SUBLANES, V7X_SUBLANES)
        return _lanes_from_rows(ref[pl.ds(row0, V7X_SUBLANES), :]).reshape(2, n, V7X_LANES)

    def time_of(s):
        return jnp.where(e == 0, s, tb_steps - 1 - s)

    def prep(s, p_prev):
        live = s < tb_steps
        t = time_of(jnp.minimum(s, tb_steps - 1))
        p_prev = jnp.where(live, p_prev, p_last[...])
        p_last[...] = p_prev
        k = load(k_ref, t)
        ag = load(ag_ref, t)
        r = load(r_ref, t)
        kk = k * kk_ref[...]
        kk = kk / jnp.maximum(jnp.sqrt(jnp.sum(kk * kk, axis=1, keepdims=True)), 1e-12)
        kd = k * (1.0 + (ag - 1.0) * ka_ref[...])
        p_new = p_prev * load(w_ref, t)
        inv = 1.0 / p_new
        r_s[t] = r * p_new
        v_s[t] = load(v_ref, t)
        a_s[t] = -kk * p_prev
        b_s[t] = kk * ag * inv
        kd_s[t] = kd * inv
        bonus_ref[t] = jnp.sum(r * kd * rk_ref[...], axis=1)
        return p_new

    def vslice(vb):
        return pl.ds(vb * V7X_SUBLANES, V7X_SUBLANES)

    zeros = tuple(jnp.zeros((V7X_SUBLANES, V7X_LANES), F32) for _ in range(nvb))

    def step(s, p):
        p = prep(s + 1, p)
        t = time_of(s)
        for par in range(2):
            def pass1(kb, acc):
                acc = list(acc)
                for kr in range(V7X_SUBLANES):
                    ki = kb * V7X_SUBLANES + kr
                    a_row = a_s[t, par, pl.ds(ki, 1), :]
                    for vb in range(nvb):
                        acc[vb] = acc[vb] + s_ref[par, ki, vslice(vb), :] * a_row
                return tuple(acc)

            sa = lax.fori_loop(0, n // V7X_SUBLANES, pass1, zeros, unroll=True)
            vv = [v_s[t, par, vslice(vb), :] for vb in range(nvb)]

            def pass2(kb, acc):
                acc = list(acc)
                for kr in range(V7X_SUBLANES):
                    ki = kb * V7X_SUBLANES + kr
                    b_row = b_s[t, par, pl.ds(ki, 1), :]
                    k_row = kd_s[t, par, pl.ds(ki, 1), :]
                    r_row = r_s[t, par, pl.ds(ki, 1), :]
                    for vb in range(nvb):
                        sn = s_ref[par, ki, vslice(vb), :] + sa[vb] * b_row + vv[vb] * k_row
                        s_ref[par, ki, vslice(vb), :] = sn
                        acc[vb] = acc[vb] + sn * r_row
                return tuple(acc)

            o = lax.fori_loop(0, n // V7X_SUBLANES, pass2, zeros, unroll=True)
            for vb in range(nvb):
                o_ref[t, par, vslice(vb), :] = o[vb]
        return p

    p0 = prep(0, jnp.ones((2, n, V7X_LANES), F32))
    w_s[...] = lax.fori_loop(0, tb_steps, step, p0)

    def rescale(ki, carry):
        for par in range(2):
            s_ref[par, ki] = s_ref[par, ki] * w_s[par, pl.ds(ki, 1), :]
        return carry

    lax.fori_loop(0, n, rescale, 0)


def _rwkv_scan(seg, r, k, v, w2, ag2, kk_p, ka_p, rk_p, s0):
    n, lanes = RWKV_HEAD, V7X_LANES
    d = r.shape[1]
    t = seg.t
    g = seg.nb // V7X_SUBLANES
    tb = min(SCAN_ROWS // seg.nb, t)
    nt = t // tb

    def tmap(e, tt):
        return jnp.where(e == 0, tt, nt - 1 - tt)

    shared = pl.BlockSpec((tb * seg.nb, d), lambda e, gg, tt: (tmap(e, tt), 0))
    per_dir = pl.BlockSpec((None, tb * seg.nb, d), lambda e, gg, tt: (e, tmap(e, tt), 0))
    par = pl.BlockSpec((2, n, lanes), lambda e, gg, tt: (0, 0, 0))
    st = pl.BlockSpec((None, None, 2, n, n, lanes), lambda e, gg, tt: (e, gg, 0, 0, 0, 0))
    o_spec = pl.BlockSpec((None, None, tb, 2, n, lanes), lambda e, gg, tt: (e, gg, tmap(e, tt), 0, 0, 0))
    bon_spec = pl.BlockSpec((None, None, tb, 2, lanes), lambda e, gg, tt: (e, gg, tmap(e, tt), 0, 0))
    scratch = pltpu.VMEM((tb, 2, n, lanes), F32)
    return pl.pallas_call(
        functools.partial(_scan_kernel, seg.nb, tb),
        grid=(2, g, nt),
        in_specs=[shared, shared, shared, per_dir, per_dir, par, par, par, st],
        out_specs=[o_spec, bon_spec, st],
        out_shape=[jax.ShapeDtypeStruct((2, g, t, 2, n, lanes), F32),
                   jax.ShapeDtypeStruct((2, g, t, 2, lanes), F32),
                   jax.ShapeDtypeStruct((2, g, 2, n, n, lanes), F32)],
        scratch_shapes=[scratch, scratch, pltpu.VMEM((2, n, lanes), F32), scratch, scratch, scratch,
                        pltpu.VMEM((2, n, lanes), F32)],
        compiler_params=_cparams(3),
        name="rwkv_scan",
    )(r, k, v, w2, ag2, kk_p, ka_p, rk_p, s0)


def _rwkv_post_kernel(nb, tb_steps, of_ref, ob_ref, v_ref, bf_ref, bb_ref, lnw_ref, lnb_ref, y_ref):
    g = pl.program_id(1)
    n = RWKV_HEAD
    n_hp = v_ref.shape[1] // V7X_LANES

    def body(t, carry):
        row0 = pl.multiple_of(t * nb + g * V7X_SUBLANES, V7X_SUBLANES)
        o = of_ref[t] + ob_ref[t]
        mean = jnp.mean(o, axis=1, keepdims=True)
        var = jnp.mean(jnp.square(o - mean), axis=1, keepdims=True)
        on = (o - mean) * lax.rsqrt(var + GN_EPS)
        v = _lanes_from_rows(v_ref[pl.ds(row0, V7X_SUBLANES), :]).reshape(2, n, V7X_LANES)
        bonus = (bf_ref[t] + bb_ref[t])[:, None, :]
        y = on * lnw_ref[...] + lnb_ref[...] + bonus * v
        rows = y.reshape(2 * n, V7X_LANES).T
        for hp in range(n_hp):
            y_ref[pl.ds(row0, V7X_SUBLANES), hp * V7X_LANES:(hp + 1) * V7X_LANES] = (
                rows[hp * V7X_SUBLANES:(hp + 1) * V7X_SUBLANES, :])
        return carry

    lax.fori_loop(0, tb_steps, body, 0)


def _rwkv_post(seg, o2, v, bonus2, lnw_p, lnb_p):
    _, g, t, _, n, lanes = o2.shape
    d = v.shape[1]
    tb = min(SCAN_ROWS // seg.nb, t)
    blk = lambda e: pl.BlockSpec((None, None, tb, 2, n, lanes), lambda tt, gg: (e, gg, tt, 0, 0, 0))
    bon = lambda e: pl.BlockSpec((None, None, tb, 2, lanes), lambda tt, gg: (e, gg, tt, 0, 0))
    par = pl.BlockSpec((2, n, lanes), lambda tt, gg: (0, 0, 0))
    nat = pl.BlockSpec((tb * seg.nb, d), lambda tt, gg: (tt, 0))
    return pl.pallas_call(
        functools.partial(_rwkv_post_kernel, seg.nb, tb),
        grid=(t // tb, g),
        in_specs=[blk(0), blk(1), nat, bon(0), bon(1), par, par],
        out_specs=nat,
        out_shape=jax.ShapeDtypeStruct((seg.m, d), F32),
        compiler_params=_cparams(2),
        name="rwkv_post",
    )(o2, o2, v, bonus2, bonus2, lnw_p, lnb_p)


def _param_lanes(p):
    h = p.size // RWKV_HEAD
    q = p.reshape(h // 2, 2, RWKV_HEAD).transpose(1, 2, 0)
    return jnp.repeat(q, V7X_LANES // (h // 2), axis=2)


def _pad_lora_in(w):
    r = w.shape[-1]
    w = jnp.pad(w, ((0, 0), (0, 0), (0, LORA_PAD - r)))
    return jnp.concatenate([w[0], w[1]], axis=1)


def _pad_lora_out(w):
    r = w.shape[1]
    z = jnp.zeros((LORA_PAD, w.shape[2]), w.dtype)
    wp = jnp.pad(w, ((0, 0), (0, LORA_PAD - r), (0, 0)))
    return jnp.stack([jnp.concatenate([wp[0], z], axis=0), jnp.concatenate([z, wp[1]], axis=0)])


def _dir_mm(seg, x, w2, bias2, epi, name):
    k = x.shape[1]
    d = w2.shape[2]
    tm = _tile_rows(seg, TM_MM)
    n_i = seg.m // tm
    tn = 512
    return pl.pallas_call(
        functools.partial(_mm_kernel, 0, 1, None, epi),
        grid=(2 * n_i, d // tn),
        in_specs=[pl.BlockSpec((tm, k), lambda i, j: (i % n_i, 0)),
                  pl.BlockSpec((None, k, tn), lambda i, j: (i // n_i, 0, j)),
                  pl.BlockSpec((None, 1, tn), lambda i, j: (i // n_i, 0, j))],
        out_specs=pl.BlockSpec((None, tm, tn), lambda i, j: (i // n_i, i % n_i, j)),
        out_shape=jax.ShapeDtypeStruct((2, seg.m, d), F32),
        compiler_params=_cparams(2),
        name=name,
    )(x, w2, bias2.reshape(2, 1, d))


def _rwkv_mixer(seg, h, x, p, state, mod, l):
    d = h.shape[1]
    xr, xw, xk, xv, xa, xg = _shift_mix(seg, h, p['mu'])
    mmf = functools.partial(_mm, seg, tn=512)
    r = mmf(xr, p['w_r'], _w2_spec(d, 512), d, out_dtype=F32, name="rwkv_r")
    k = mmf(xk, p['w_k'], _w2_spec(d, 512), d, out_dtype=F32, name="rwkv_k")
    v = mmf(xv, p['w_v'], _w2_spec(d, 512), d, out_dtype=F32, name="rwkv_v")
    gl = _mm(seg, xg, p['g1'], _w2_spec(d, 256), 256, tn=256, out_dtype=BF16, name="rwkv_g1", epi=_epi_sigmoid)
    gate = mmf(gl, p['g2'], _w2_spec(256, 512), d, out_dtype=F32, name="rwkv_g2")
    nl = 2 * LORA_PAD
    wl = _mm(seg, xw, p['w1p'], _w2_spec(d, nl), nl, tn=nl, out_dtype=BF16, name="rwkv_w1", epi=_epi_tanh)
    al = _mm(seg, xa, p['a1p'], _w2_spec(d, nl), nl, tn=nl, out_dtype=BF16, name="rwkv_a1")
    dec2 = _dir_mm(seg, wl, p['w2p'], p['w0'], _epi_decay, "rwkv_w2")
    ag2 = _dir_mm(seg, al, p['a2p'], p['a0'], _epi_bias_sigmoid, "rwkv_a2")

    n_heads = d // RWKV_HEAD
    n = RWKV_HEAD
    g = seg.nb // V7X_SUBLANES
    if state is None:
        s0 = jnp.zeros((2, g, 2, n, n, V7X_LANES), F32)
    else:
        s0 = state.reshape(g, V7X_SUBLANES, 2, n_heads // 2, 2, n, n).transpose(2, 0, 4, 6, 5, 3, 1)
        s0 = s0.reshape(2, g, 2, n, n, V7X_LANES)
    o2, bonus2, sfin = _rwkv_scan(seg, r, k, v, dec2, ag2, p['kk_p'], p['ka_p'], p['rk_p'], s0)
    y = _rwkv_post(seg, o2, v, bonus2, p['lnw_p'], p['lnb_p'])
    x_new = _mm(seg, y, p['w_o'], _w2_spec(d, 512), d, tn=512, tm=TM_ROW, out_dtype=F32, name="rwkv_wo",
                pro=_pro_mul, pro_args=(('full', gate),),
                epi=_epi_resid, epi_args=(('tile', x), ('mod', mod, l, 2)))
    return x_new, sfin


def _head_rmsnorm(acc, w_ref):
    outs = []
    for hh in range(acc.shape[1] // ATTN_HEAD_DIM):
        xh = acc[:, hh * ATTN_HEAD_DIM:(hh + 1) * ATTN_HEAD_DIM]
        yh = xh * lax.rsqrt(jnp.mean(xh * xh, axis=-1, keepdims=True) + NORM_EPS)
        outs.append(yh * w_ref[...])
    return jnp.concatenate(outs, axis=1) if len(outs) > 1 else outs[0]


def _rope(y, cos_ref, sin_ref):
    nh = y.shape[1] // ATTN_HEAD_DIM
    cos = jnp.concatenate([cos_ref[...]] * nh, axis=1) if nh > 1 else cos_ref[...]
    sin = jnp.concatenate([sin_ref[...]] * nh, axis=1) if nh > 1 else sin_ref[...]
    width = y.shape[1]
    lane = lax.broadcasted_iota(jnp.int32, (1, width), 1)
    quarter = ATTN_HEAD_DIM // 4
    first = (lane & (2 * quarter - 1)) < quarter
    swapped = jnp.where(first, pltpu.roll(y, width - quarter, axis=1), pltpu.roll(y, quarter, axis=1))
    return y * cos + swapped * sin


def _epi_qk_rope(acc, nw_ref, cos_ref, sin_ref):
    return _rope(_head_rmsnorm(acc, nw_ref), cos_ref, sin_ref)


def _epi_qk(acc, nw_ref):
    return _head_rmsnorm(acc, nw_ref)


def _rope_tables(seg):
    t = seg.t
    row = (jnp.arange(t) // GRID_W).astype(F32)
    col = (jnp.arange(t) % GRID_W).astype(F32)
    n_freq = ATTN_HEAD_DIM // 4
    inv = ROPE_BASE ** (-jnp.arange(n_freq, dtype=F32) / n_freq)
    ar, ac = row[:, None] * inv, col[:, None] * inv
    cos = jnp.concatenate([jnp.cos(ar), jnp.cos(ar), jnp.cos(ac), jnp.cos(ac)], axis=1)
    sin = jnp.concatenate([-jnp.sin(ar), jnp.sin(ar), -jnp.sin(ac), jnp.sin(ac)], axis=1)
    return jnp.repeat(cos, seg.nb, axis=0), jnp.repeat(sin, seg.nb, axis=0)


def _attn_kernel(seq_nb, n_band, nblk, has_cache, *refs):
    n_tm = n_band if n_band else 1
    n_src = n_tm + (1 if has_cache else 0)
    q_refs = refs[:ATTN_GROUP]
    refs = refs[ATTN_GROUP:]
    k_refs = refs[:n_src]
    v_refs = refs[n_src:2 * n_src]
    sink_ref = refs[2 * n_src]
    o_refs = refs[2 * n_src + 1:]
    blk = pl.program_id(1)
    bq = q_refs[0].shape[0] // seq_nb

    def src(refs_, si, b):
        if si < n_tm:
            rows = refs_[si].shape[0] // seq_nb
            return refs_[si][pl.ds(b, rows, stride=seq_nb), :]
        return refs_[si][b]

    for b in range(seq_nb):
        q = jnp.concatenate([q_ref[pl.ds(b, bq, stride=seq_nb), :] for q_ref in q_refs], axis=0).astype(BF16)
        scores = []
        for si in range(n_src):
            s = lax.dot_general(q, src(k_refs, si, b).astype(BF16), (((1,), (1,)), ((), ())),
                                preferred_element_type=F32) * ATTN_SCALE
            if n_band == 3 and si in (0, 2):
                qi = lax.broadcasted_iota(jnp.int32, (ATTN_GROUP * bq, bq), 0) & (bq - 1)
                ki = lax.broadcasted_iota(jnp.int32, (ATTN_GROUP * bq, bq), 1)
                if si == 0:
                    valid, present = ki >= qi, blk > 0
                else:
                    valid, present = ki <= qi, blk < nblk - 1
                s = jnp.where(valid, s, NEG_INF) + jnp.where(present, 0.0, NEG_INF)
            scores.append(s)
        sink = sink_ref[...]

        def lane_tiles(a):
            return [a[:, c:c + V7X_LANES] for c in range(0, a.shape[1], V7X_LANES)]

        m = jnp.maximum(sink, jnp.max(functools.reduce(jnp.maximum, [c for s in scores for c in lane_tiles(s)]),
                                      axis=-1, keepdims=True))
        ps = [jnp.exp(s - m) for s in scores]
        den = jnp.exp(sink - m) + jnp.sum(functools.reduce(jnp.add, [c for p in ps for c in lane_tiles(p)]),
                                          axis=-1, keepdims=True)
        inv = 1.0 / den
        out = None
        for si in range(n_src):
            o = jnp.dot((ps[si] * inv).astype(BF16), src(v_refs, si, b).astype(BF16), preferred_element_type=F32)
            out = o if out is None else out + o
        for gq, o_ref in enumerate(o_refs):
            o_ref[pl.ds(b, bq, stride=seq_nb), :] = out[gq * bq:(gq + 1) * bq, :]


def _pro_heads(x0, *other_refs):
    parts = [x0] + [r[...] for r in other_refs]
    return jnp.concatenate([parts[gq][kv] for kv in range(x0.shape[0]) for gq in range(len(parts))], axis=1)


def _attn_mixer(seg, h, x, p, cache_k, cache_v, mod, l):
    d = h.shape[1]
    dh = ATTN_HEAD_DIM
    nkv = p['w_qkv'].shape[1] // dh // (ATTN_GROUP + 2)
    nq = nkv * ATTN_GROUP
    tn = 512
    norm_arg = lambda w: ('raw', w.reshape(1, dh), pl.BlockSpec((1, dh), lambda i, j: (0, 0)))
    if cache_k is None:
        epi, extra = _epi_qk, ()
    else:
        cos, sin = _rope_tables(seg)
        tm_q = _tile_rows(seg, TM_MM)
        tab = lambda a: ('raw', a, pl.BlockSpec((tm_q, dh), lambda i, j: (i, 0)))
        epi, extra = _epi_qk_rope, (tab(cos), tab(sin))
    q = _mm(seg, h, p['w_qkv'], _w2_spec(d, tn), nq * dh, tn=tn, out_dtype=F32, name="attn_q", epi=epi,
            epi_args=(norm_arg(p['q_norm']),) + extra, head_dim_out=dh)
    k = _mm(seg, h, p['w_qkv'], _w2_spec(d, tn, nq * dh // tn), nkv * dh, tn=tn, out_dtype=F32, name="attn_k",
            epi=epi, epi_args=(norm_arg(p['k_norm']),) + extra)
    v = _mm(seg, h, p['w_qkv'], _w2_spec(d, tn, (nq + nkv) * dh // tn), nkv * dh, tn=tn, out_dtype=F32,
            name="attn_v")

    bq = ATTN_BLOCK
    nblk = seg.t // bq
    rows = bq * seg.nb
    sink_col = jnp.repeat(p['sink'].reshape(nkv, ATTN_GROUP), bq, axis=1).reshape(nkv, ATTN_GROUP * bq, 1)
    sink_spec = pl.BlockSpec((None, ATTN_GROUP * bq, 1), lambda kv, qb: (kv, 0, 0))
    q_specs = [pl.BlockSpec((None, rows, dh), lambda kv, qb, gq=gq: (kv * ATTN_GROUP + gq, qb, 0))
               for gq in range(ATTN_GROUP)]
    q_args = (q,) * ATTN_GROUP
    if cache_k is None:
        full = pl.BlockSpec((seg.m, dh), lambda kv, qb: (0, kv))
        in_specs = [*q_specs, full, full, sink_spec]
        args = (*q_args, k, v, sink_col)
        n_band = 0
    else:
        past = cache_k.shape[1]
        ck = cache_k.reshape(seg.nb, past, nkv * dh)
        cv = cache_v.reshape(seg.nb, past, nkv * dh)

        def band(delta):
            return pl.BlockSpec((rows, dh), lambda kv, qb: (jnp.clip(qb + delta, 0, nblk - 1), kv))

        cache_spec = pl.BlockSpec((seg.nb, past, dh), lambda kv, qb: (0, 0, kv))
        in_specs = [*q_specs, band(-1), band(0), band(1), cache_spec, band(-1), band(0), band(1), cache_spec,
                    sink_spec]
        args = (*q_args, k, k, k, ck, v, v, v, cv, sink_col)
        n_band = 3
    o_parts = pl.pallas_call(
        functools.partial(_attn_kernel, seg.nb, n_band, nblk, cache_k is not None),
        grid=(nkv, nblk),
        in_specs=in_specs,
        out_specs=[pl.BlockSpec((None, rows, dh), lambda kv, qb: (kv, qb, 0))] * ATTN_GROUP,
        out_shape=[jax.ShapeDtypeStruct((nkv, seg.m, dh), F32)] * ATTN_GROUP,
        compiler_params=_cparams(2),
        name="attn_core",
    )(*args)

    tm_o = _tile_rows(seg, TM_MM)
    o_spec = pl.BlockSpec((nkv, tm_o, dh), lambda i, j: (0, i, 0))
    x_new = _mm(seg, o_parts[0], p['w_o'], _w2_spec(d, 512), d, tn=512, tm=TM_MM, out_dtype=F32, name="attn_wo",
                k=nq * dh, x_spec=o_spec, pro=_pro_heads, pro_args=tuple(('raw', a, o_spec) for a in o_parts[1:]),
                epi=_epi_resid, epi_args=(('tile', x), ('mod', mod, l, 2)))
    return x_new, k, v


def _ffn_up_kernel(nb, n_i, x_ref, xp_ref, xn_ref, wg_ref, wv_ref, cwg_ref, cwv_ref, cbg_ref, cbv_ref, o_ref):
    tm = x_ref.shape[0]
    halo = xp_ref.shape[0]
    xh = _with_halo(x_ref, xp_ref, xn_ref, n_i)

    def conv(w_ref, cw_ref, cb_ref):
        u = jnp.dot(xh, w_ref[...].astype(BF16), preferred_element_type=F32)
        prev = u[halo - nb:halo - nb + tm]
        cur = u[halo:halo + tm]
        nxt = u[halo + nb:halo + nb + tm]
        return prev * cw_ref[0:1, :] + cur * cw_ref[1:2, :] + nxt * cw_ref[2:3, :] + cb_ref[...]

    gate = conv(wg_ref, cwg_ref, cbg_ref)
    val = conv(wv_ref, cwv_ref, cbv_ref)
    o_ref[...] = (gate * jax.nn.sigmoid(gate) * val).astype(o_ref.dtype)


def _conv_ffn(seg, h, x, up, conv_w, conv_b, down, mod, l):
    d = h.shape[1]
    f = down.shape[1]
    tn = 512
    nj = f // tn
    tm = _tile_rows(seg, TM_FFN)
    n_i = seg.m // tm
    halo = max(V7X_BF16_ROWS, seg.nb)
    hp, hn = _halo_specs(tm, n_i, halo, d, lambda j: 0)
    cw = lambda off: pl.BlockSpec((None, conv_w.shape[1], tn), lambda i, j: (l, 0, j + off))
    cb = lambda off: pl.BlockSpec((None, 1, tn), lambda i, j: (l, 0, j + off))
    cb3 = conv_b.reshape(conv_b.shape[0], 1, 2 * f)
    act = pl.pallas_call(
        functools.partial(_ffn_up_kernel, seg.nb, n_i),
        grid=(n_i, nj),
        in_specs=[_full_spec(tm, d), hp, hn, _w3_spec(l, d, tn), _w3_spec(l, d, tn, nj), cw(0), cw(nj), cb(0), cb(nj)],
        out_specs=_tile_spec(tm, tn),
        out_shape=jax.ShapeDtypeStruct((seg.m, f), BF16),
        compiler_params=_cparams(2),
        name="ffn_up",
    )(h, h, h, up, up, conv_w, conv_w, cb3, cb3)
    return _mm(seg, act, down, _w3_spec(l, f, 256), d, tn=256, out_dtype=F32, name="ffn_down",
               epi=_epi_resid, epi_args=(('tile', x), ('mod', mod, l, 5)))


def kernel(x_prompt, x_sample, state_rwkv, cache_k, cache_v, c, c_ctx, ada_w, ada_b, norm_mix, norm_ffn, ffn_up, ffn_conv_w, ffn_conv_b, ffn_down, pool_w, pool_scale, rwkv_mu, rwkv_w_r, rwkv_w_k, rwkv_w_v, rwkv_w0, rwkv_w1, rwkv_w2, rwkv_a0, rwkv_a1, rwkv_a2, rwkv_g1, rwkv_g2, rwkv_k_k, rwkv_k_a, rwkv_r_k, rwkv_ln_w, rwkv_ln_b, rwkv_w_o, attn_w_qkv, attn_q_norm, attn_k_norm, attn_sink, attn_w_o):
    n_ctx, l_ctx, d = x_prompt.shape
    n_lat, l_lat, _ = x_sample.shape
    depth = ada_w.shape[0]
    assert n_ctx % V7X_SUBLANES == 0 and n_lat == V7X_SUBLANES, "8 latent sequences share one conditioning tile"
    assert d // RWKV_HEAD // 2 * V7X_SUBLANES == V7X_LANES, "8 sequences x head pairs must fill the lanes"

    n_cond = n_lat + 1
    ncp = -(-n_cond // V7X_SUBLANES) * V7X_SUBLANES
    cond = jnp.concatenate([c, c_ctx[None, :], jnp.zeros((ncp - n_cond, d), F32)], axis=0)
    mod = _adaln(cond, ada_w, ada_b)
    mod4 = mod.reshape(depth, ncp, 6, d).transpose(0, 2, 1, 3)
    mods = (mod4[:, :, n_lat:n_lat + 1], mod4[:, :, :n_lat])

    segs = (_Seg(n_ctx, l_ctx, 1), _Seg(n_lat, l_lat, n_lat))
    xs = [a.transpose(1, 0, 2).reshape(a.shape[0] * a.shape[1], d) for a in (x_prompt, x_sample)]

    n_mixers = 3
    slots = [0] * n_mixers
    new_state, new_k, new_v = [], [], []
    for l in range(depth):
        kind = l % n_mixers
        slot = slots[kind]
        slots[kind] += 1
        if kind == 1:
            p = {'mu': rwkv_mu[slot], 'w_r': rwkv_w_r[slot], 'w_k': rwkv_w_k[slot], 'w_v': rwkv_w_v[slot],
                 'w0': rwkv_w0[slot], 'w1p': _pad_lora_in(rwkv_w1[slot]), 'w2p': _pad_lora_out(rwkv_w2[slot]),
                 'a0': rwkv_a0[slot], 'a1p': _pad_lora_in(rwkv_a1[slot]), 'a2p': _pad_lora_out(rwkv_a2[slot]),
                 'g1': rwkv_g1[slot], 'g2': rwkv_g2[slot], 'w_o': rwkv_w_o[slot],
                 'kk_p': _param_lanes(rwkv_k_k[slot]), 'ka_p': _param_lanes(rwkv_k_a[slot]),
                 'rk_p': _param_lanes(rwkv_r_k[slot]), 'lnw_p': _param_lanes(rwkv_ln_w[slot]),
                 'lnb_p': _param_lanes(rwkv_ln_b[slot])}
        elif kind == 2:
            p = {'w_qkv': attn_w_qkv[slot], 'q_norm': attn_q_norm[slot], 'k_norm': attn_k_norm[slot],
                 'sink': attn_sink[slot], 'w_o': attn_w_o[slot]}
        for si, seg in enumerate(segs):
            mod_s = mods[si]
            h_dtype = BF16 if kind == 2 else F32
            x = xs[si]
            h = _norm_mod(seg, x, norm_mix, mod_s, l, 0, 1, h_dtype)
            if kind == 0:
                x = _pool_mixer(seg, h, x, pool_w[slot], pool_scale[slot], mod_s, l)
            elif kind == 1:
                x, sfin = _rwkv_mixer(seg, h, x, p, state_rwkv[:, slot] if si == 1 else None, mod_s, l)
                if si == 0:
                    g = seg.nb // V7X_SUBLANES
                    hp = d // RWKV_HEAD // 2
                    st = sfin.reshape(2, g, 2, RWKV_HEAD, RWKV_HEAD, hp, V7X_SUBLANES).transpose(1, 6, 0, 5, 2, 4, 3)
                    new_state.append(st.reshape(seg.nb, 2, 2 * hp, RWKV_HEAD, RWKV_HEAD))
            else:
                x, k, v = _attn_mixer(seg, h, x, p, cache_k[:, slot] if si == 1 else None,
                                      cache_v[:, slot] if si == 1 else None, mod_s, l)
                if si == 0:
                    nkv = k.shape[1] // ATTN_HEAD_DIM
                    to_bm = lambda a: a.reshape(seg.t, seg.nb, nkv, ATTN_HEAD_DIM).transpose(1, 0, 2, 3)
                    new_k.append(to_bm(k))
                    new_v.append(to_bm(v))
            h2 = _norm_mod(seg, x, norm_ffn, mod_s, l, 3, 4, BF16)
            xs[si] = _conv_ffn(seg, h2, x, ffn_up, ffn_conv_w, ffn_conv_b, ffn_down, mod_s, l)

    ys = [a.reshape(seg.t, seg.nb, d).transpose(1, 0, 2) for a, seg in zip(xs, segs)]
    return (ys[0], ys[1], jnp.stack(new_state, axis=1), jnp.stack(new_k, axis=1), jnp.stack(new_v, axis=1))
```

```python
import functools
from typing import NamedTuple

import jax
import jax.numpy as jnp
from jax import lax
from jax.experimental import pallas as pl
from jax.experimental.pallas import tpu as pltpu

F32 = jnp.float32
BF16 = jnp.bfloat16

V7X_LANES = 128
V7X_SUBLANES = 8
V7X_BF16_ROWS = 16
V7X_VMEM_LIMIT_BYTES = 56 * 1024 * 1024

NORM_EPS = 1e-6
GN_EPS = 64e-5
DECAY_SCALE = 0.606531
POOL_WINDOWS = (2, 4, 8, 16)
RWKV_HEAD = 64
ATTN_HEAD_DIM = 128
ATTN_GROUP = 4
ATTN_BLOCK = 128
GRID_W = 64
ROPE_BASE = 10000.0
ATTN_SCALE = ATTN_HEAD_DIM ** -0.5
NEG_INF = -1e30
LORA_PAD = 128

TM_FFN = 2048
TM_MM = 1024
TM_ROW = 512
SCAN_ROWS = 256


class _Seg(NamedTuple):
    nb: int
    t: int
    mod_rows: int

    @property
    def m(self):
        return self.nb * self.t


def _cparams(n_axes):
    return pltpu.CompilerParams(dimension_semantics=("arbitrary",) * n_axes,
                                vmem_limit_bytes=V7X_VMEM_LIMIT_BYTES)


def _tile_rows(seg, target):
    return min(target, seg.m)


def _full_spec(tm, w):
    return pl.BlockSpec((tm, w), lambda i, j: (i, 0))


def _tile_spec(tm, tn):
    return pl.BlockSpec((tm, tn), lambda i, j: (i, j))


def _mod_spec(seg, l, which, tn):
    return pl.BlockSpec((None, None, seg.mod_rows, tn), lambda i, j: (l, which, 0, j))


def _per_row(fn, y, m):
    if m.shape[0] == 1:
        return fn(y, m)
    tm, w = y.shape
    return fn(y.reshape(tm // m.shape[0], m.shape[0], w), m[None]).reshape(tm, w)


def _halo_specs(tm, n_i, rows, width_block, col_fn):
    per = tm // rows
    n_blocks = n_i * per
    prev = pl.BlockSpec((rows, width_block), lambda i, j: (jnp.maximum(i * per - 1, 0), col_fn(j)))
    nxt = pl.BlockSpec((rows, width_block), lambda i, j: (jnp.minimum((i + 1) * per, n_blocks - 1), col_fn(j)))
    return prev, nxt


def _with_halo(x_ref, prev_ref, next_ref, n_i):
    i = pl.program_id(0)
    prev = jnp.where(i > 0, prev_ref[...], jnp.zeros_like(prev_ref[...]))
    nxt = jnp.where(i < n_i - 1, next_ref[...], jnp.zeros_like(next_ref[...]))
    return jnp.concatenate([prev, x_ref[...], nxt], axis=0)


def _adaln_kernel(c_ref, w_ref, b_ref, o_ref):
    c = c_ref[...]
    s = c * jax.nn.sigmoid(c)
    acc = jnp.dot(s.astype(BF16), w_ref[...].astype(BF16), preferred_element_type=F32)
    o_ref[...] = acc + b_ref[...]


def _adaln(cond, ada_w, ada_b):
    depth, d, n6 = ada_w.shape
    ncp = cond.shape[0]
    tn = 1024
    return pl.pallas_call(
        _adaln_kernel,
        grid=(depth, n6 // tn),
        in_specs=[pl.BlockSpec((ncp, d), lambda l, j: (0, 0)),
                  pl.BlockSpec((None, d, tn), lambda l, j: (l, 0, j)),
                  pl.BlockSpec((None, 1, tn), lambda l, j: (l, 0, j))],
        out_specs=pl.BlockSpec((None, ncp, tn), lambda l, j: (l, 0, j)),
        out_shape=jax.ShapeDtypeStruct((depth, ncp, n6), F32),
        compiler_params=_cparams(2),
        name="adaln",
    )(cond, ada_w, ada_b.reshape(depth, 1, n6))


def _norm_mod_math(x, g, sh, sc):
    y = x * lax.rsqrt(jnp.mean(x * x, axis=-1, keepdims=True) + NORM_EPS)
    y = y * g
    return _per_row(lambda a, b: a * (1.0 + b), y, sc), sh


def _norm_mod_kernel(x_ref, g_ref, sh_ref, sc_ref, o_ref):
    y, sh = _norm_mod_math(x_ref[...], g_ref[...], sh_ref[...], sc_ref[...])
    o_ref[...] = _per_row(lambda a, b: a + b, y, sh).astype(o_ref.dtype)


def _norm_mod(seg, x, g, mod, l, shift_i, scale_i, out_dtype):
    d = g.shape[-1]
    tm = _tile_rows(seg, TM_ROW)
    return pl.pallas_call(
        _norm_mod_kernel,
        grid=(seg.m // tm, 1),
        in_specs=[_full_spec(tm, d), pl.BlockSpec((None, 1, d), lambda i, j: (l, 0, 0)),
                  _mod_spec(seg, l, shift_i, d), _mod_spec(seg, l, scale_i, d)],
        out_specs=_full_spec(tm, d),
        out_shape=jax.ShapeDtypeStruct((seg.m, d), out_dtype),
        compiler_params=_cparams(2),
        name="norm_mod",
    )(x, g.reshape(g.shape[0], 1, d), mod, mod)


def _mm_kernel(n_pro, n_epi, pro, epi, x_ref, *refs):
    pro_refs = refs[:n_pro]
    w_ref = refs[n_pro]
    epi_refs = refs[n_pro + 1:n_pro + 1 + n_epi]
    o_ref = refs[n_pro + 1 + n_epi]
    x = x_ref[...]
    if pro is not None:
        x = pro(x, *pro_refs)
    acc = jnp.dot(x.astype(BF16), w_ref[...].astype(BF16), preferred_element_type=F32)
    if epi is not None:
        acc = epi(acc, *epi_refs)
    if len(o_ref.shape) == 3:
        hd = o_ref.shape[2]
        for hh in range(o_ref.shape[0]):
            o_ref[hh] = acc[:, hh * hd:(hh + 1) * hd].astype(o_ref.dtype)
    else:
        o_ref[...] = acc.astype(o_ref.dtype)


def _mm(seg, x, w, w_spec, n, *, tn, out_dtype, name, tm=None, k=None, x_spec=None, pro=None, pro_args=(),
        epi=None, epi_args=(), head_dim_out=None):
    k = x.shape[1] if k is None else k
    tm = _tile_rows(seg, TM_MM if tm is None else tm)

    def lower(arg):
        kind = arg[0]
        if kind == 'full':
            return arg[1], _full_spec(tm, arg[1].shape[1])
        if kind == 'tile':
            return arg[1], _tile_spec(tm, tn)
        if kind == 'row':
            return arg[1], pl.BlockSpec((1, tn), lambda i, j: (0, j))
        if kind == 'mod':
            return arg[1], _mod_spec(seg, arg[2], arg[3], tn)
        return arg[1], arg[2]

    pro_l = [lower(a) for a in pro_args]
    epi_l = [lower(a) for a in epi_args]
    if head_dim_out is None:
        out_spec = _tile_spec(tm, tn)
        out_shape = jax.ShapeDtypeStruct((seg.m, n), out_dtype)
    else:
        out_spec = pl.BlockSpec((tn // head_dim_out, tm, head_dim_out), lambda i, j: (j, i, 0))
        out_shape = jax.ShapeDtypeStruct((n // head_dim_out, seg.m, head_dim_out), out_dtype)
    return pl.pallas_call(
        functools.partial(_mm_kernel, len(pro_l), len(epi_l), pro, epi),
        grid=(seg.m // tm, n // tn),
        in_specs=[_full_spec(tm, k) if x_spec is None else x_spec, *[sp for _, sp in pro_l], w_spec,
                  *[sp for _, sp in epi_l]],
        out_specs=out_spec,
        out_shape=out_shape,
        compiler_params=_cparams(2),
        name=name,
    )(x, *[a for a, _ in pro_l], w, *[a for a, _ in epi_l])


def _w2_spec(k, tn, joff=0):
    return pl.BlockSpec((k, tn), lambda i, j: (0, j + joff))


def _w3_spec(l, k, tn, joff=0):
    return pl.BlockSpec((None, k, tn), lambda i, j: (l, 0, j + joff))


def _epi_resid(acc, x_ref, gate_ref):
    return x_ref[...] + _per_row(lambda a, b: a * b, acc, gate_ref[...])


def _epi_sigmoid(acc):
    return jax.nn.sigmoid(acc)


def _epi_tanh(acc):
    return jnp.tanh(acc)


def _epi_decay(acc, w0_ref):
    return jnp.exp(-DECAY_SCALE * jax.nn.sigmoid(w0_ref[...] + acc))


def _epi_bias_sigmoid(acc, a0_ref):
    return jax.nn.sigmoid(a0_ref[...] + acc)


def _pro_mul(x, g_ref):
    return x * g_ref[...]


def _pool_kernel(seg, n_i, h_ref, hp_ref, hn_ref, x_ref, w_ref, sc_ref, gate_ref, o_ref):
    i = pl.program_id(0)
    g = pl.program_id(1)
    tm = h_ref.shape[0]
    nb = seg.nb
    halo = hp_ref.shape[0]
    steps = tm // nb
    t_abs = i * steps + lax.broadcasted_iota(jnp.int32, (tm, 1), 0) // nb

    for gi, win in enumerate(POOL_WINDOWS):
        @pl.when(g == gi)
        def _(win=win):
            left = win // 2
            right = win - 1 - left
            ext = _with_halo(h_ref, hp_ref, hn_ref, n_i)
            acc, width = ext, 1
            while width < win:
                shift = width * nb
                acc = acc[shift:] + acc[:acc.shape[0] - shift]
                width *= 2
            start = halo - left * nb
            total = acc[start:start + tm]
            cnt = (jnp.minimum(t_abs + right + 1, seg.t) - jnp.maximum(t_abs - left, 0)).astype(F32)
            h = h_ref[...]
            pooled = total / cnt - h
            y = jnp.dot(pooled.astype(BF16), w_ref[...].astype(BF16), preferred_element_type=F32) * sc_ref[...]
            o_ref[...] = x_ref[...] + _per_row(lambda a, b: a * b, y, gate_ref[...])


def _pool_mixer(seg, h, x, w_grp, scale, mod, l):
    ngrp, gsz, _ = w_grp.shape
    d = h.shape[1]
    tm = _tile_rows(seg, TM_MM)
    n_i = seg.m // tm
    halo = max(POOL_WINDOWS) // 2 * seg.nb
    hp, hn = _halo_specs(tm, n_i, halo, gsz, lambda j: j)
    return pl.pallas_call(
        functools.partial(_pool_kernel, seg, n_i),
        grid=(n_i, ngrp),
        in_specs=[_tile_spec(tm, gsz), hp, hn, _tile_spec(tm, gsz),
                  pl.BlockSpec((None, gsz, gsz), lambda i, j: (j, 0, 0)),
                  pl.BlockSpec((1, gsz), lambda i, j: (0, j)),
                  _mod_spec(seg, l, 2, gsz)],
        out_specs=_tile_spec(tm, gsz),
        out_shape=jax.ShapeDtypeStruct((seg.m, d), F32),
        compiler_params=_cparams(2),
        name="pool_mixer",
    )(h, h, h, x, w_grp, scale.reshape(1, d), mod)


def _shift_mix_kernel(nb, n_i, h_ref, hp_ref, hn_ref, mu_ref, *o_refs):
    tm = h_ref.shape[0]
    ext = _with_halo(h_ref, hp_ref, hn_ref, n_i)
    h = h_ref[...]
    xx = 0.5 * (ext[0:tm] + ext[2 * nb:2 * nb + tm]) - h
    for n, o_ref in enumerate(o_refs):
        o_ref[...] = (h + xx * mu_ref[n:n + 1, :]).astype(o_ref.dtype)


def _shift_mix(seg, h, mu):
    d = h.shape[1]
    tn = 512
    n_out = mu.shape[0]
    tm = _tile_rows(seg, TM_MM)
    n_i = seg.m // tm
    hp, hn = _halo_specs(tm, n_i, seg.nb, tn, lambda j: j)
    return pl.pallas_call(
        functools.partial(_shift_mix_kernel, seg.nb, n_i),
        grid=(n_i, d // tn),
        in_specs=[_tile_spec(tm, tn), hp, hn, pl.BlockSpec((n_out, tn), lambda i, j: (0, j))],
        out_specs=[_tile_spec(tm, tn)] * n_out,
        out_shape=[jax.ShapeDtypeStruct((seg.m, d), BF16)] * n_out,
        compiler_params=_cparams(2),
        name="rwkv_shift_mix",
    )(h, h, h, mu)


def _lanes_from_rows(rows):
    n_hp = rows.shape[1] // V7X_LANES
    stacked = jnp.concatenate([rows[:, hp * V7X_LANES:(hp + 1) * V7X_LANES] for hp in range(n_hp)], axis=0)
    return stacked.T


def _scan_kernel(nb, tb_steps, r_ref, k_ref, v_ref, w_ref, ag_ref, kk_ref, ka_ref, rk_ref, s0_ref,
                 o_ref, bonus_ref, s_ref, r_s, v_s, w_s, a_s, b_s, kd_s, p_last):
    e = pl.program_id(0)
    g = pl.program_id(1)
    tb = pl.program_id(2)
    n = RWKV_HEAD
    nvb = n // V7X_SUBLANES

    @pl.when(tb == 0)
    def _():
        s_ref[...] = s0_ref[...]

    def load(ref, t):
        row0 = pl.multiple_of(t * nb + g * V7X_SUBLANES, V7X_SUBLANES)
        return _lanes_from_rows(ref[pl.ds(row0, V7X_SUBLANES), :]).reshape(2, n, V7X_LANES)

    def time_of(s):
        return jnp.where(e == 0, s, tb_steps - 1 - s)

    def prep(s, p_prev):
        live = s < tb_steps
        t = time_of(jnp.minimum(s, tb_steps - 1))
        p_prev = jnp.where(live, p_prev, p_last[...])
        p_last[...] = p_prev
        k = load(k_ref, t)
        ag = load(ag_ref, t)
        r = load(r_ref, t)
        kk = k * kk_ref[...]
        kk = kk / jnp.maximum(jnp.sqrt(jnp.sum(kk * kk, axis=1, keepdims=True)), 1e-12)
        kd = k * (1.0 + (ag - 1.0) * ka_ref[...])
        p_new = p_prev * load(w_ref, t)
        inv = 1.0 / p_new
        r_s[t] = r * p_new
        v_s[t] = load(v_ref, t)
        a_s[t] = -kk * p_prev
        b_s[t] = kk * ag * inv
        kd_s[t] = kd * inv
        bonus_ref[t] = jnp.sum(r * kd * rk_ref[...], axis=1)
        return p_new

    def vslice(vb):
        return pl.ds(vb * V7X_SUBLANES, V7X_SUBLANES)

    zeros = tuple(jnp.zeros((V7X_SUBLANES, V7X_LANES), F32) for _ in range(nvb))

    def step(s, p):
        p = prep(s + 1, p)
        t = time_of(s)
        for par in range(2):
            def pass1(kb, acc):
                acc = list(acc)
                for kr in range(V7X_SUBLANES):
                    ki = kb * V7X_SUBLANES + kr
                    a_row = a_s[t, par, pl.ds(ki, 1), :]
                    for vb in range(nvb):
                        acc[vb] = acc[vb] + s_ref[par, ki, vslice(vb), :] * a_row
                return tuple(acc)

            sa = lax.fori_loop(0, n // V7X_SUBLANES, pass1, zeros, unroll=True)
            vv = [v_s[t, par, vslice(vb), :] for vb in range(nvb)]

            def pass2(kb, acc):
                acc = list(acc)
                for kr in range(V7X_SUBLANES):
                    ki = kb * V7X_SUBLANES + kr
                    b_row = b_s[t, par, pl.ds(ki, 1), :]
                    k_row = kd_s[t, par, pl.ds(ki, 1), :]
                    r_row = r_s[t, par, pl.ds(ki, 1), :]
                    for vb in range(nvb):
                        sn = s_ref[par, ki, vslice(vb), :] + sa[vb] * b_row + vv[vb] * k_row
                        s_ref[par, ki, vslice(vb), :] = sn
                        acc[vb] = acc[vb] + sn * r_row
                return tuple(acc)

            o = lax.fori_loop(0, n // V7X_SUBLANES, pass2, zeros, unroll=True)
            for vb in range(nvb):
                o_ref[t, par, vslice(vb), :] = o[vb]
        return p

    p0 = prep(0, jnp.ones((2, n, V7X_LANES), F32))
    w_s[...] = lax.fori_loop(0, tb_steps, step, p0)

    def rescale(ki, carry):
        for par in range(2):
            s_ref[par, ki] = s_ref[par, ki] * w_s[par, pl.ds(ki, 1), :]
        return carry

    lax.fori_loop(0, n, rescale, 0)


def _rwkv_scan(seg, r, k, v, w2, ag2, kk_p, ka_p, rk_p, s0):
    n, lanes = RWKV_HEAD, V7X_LANES
    d = r.shape[1]
    t = seg.t
    g = seg.nb // V7X_SUBLANES
    tb = min(SCAN_ROWS // seg.nb, t)
    nt = t // tb

    def tmap(e, tt):
        return jnp.where(e == 0, tt, nt - 1 - tt)

    shared = pl.BlockSpec((tb * seg.nb, d), lambda e, gg, tt: (tmap(e, tt), 0))
    per_dir = pl.BlockSpec((None, tb * seg.nb, d), lambda e, gg, tt: (e, tmap(e, tt), 0))
    par = pl.BlockSpec((2, n, lanes), lambda e, gg, tt: (0, 0, 0))
    st = pl.BlockSpec((None, None, 2, n, n, lanes), lambda e, gg, tt: (e, gg, 0, 0, 0, 0))
    o_spec = pl.BlockSpec((None, None, tb, 2, n, lanes), lambda e, gg, tt: (e, gg, tmap(e, tt), 0, 0, 0))
    bon_spec = pl.BlockSpec((None, None, tb, 2, lanes), lambda e, gg, tt: (e, gg, tmap(e, tt), 0, 0))
    scratch = pltpu.VMEM((tb, 2, n, lanes), F32)
    return pl.pallas_call(
        functools.partial(_scan_kernel, seg.nb, tb),
        grid=(2, g, nt),
        in_specs=[shared, shared, shared, per_dir, per_dir, par, par, par, st],
        out_specs=[o_spec, bon_spec, st],
        out_shape=[jax.ShapeDtypeStruct((2, g, t, 2, n, lanes), F32),
                   jax.ShapeDtypeStruct((2, g, t, 2, lanes), F32),
                   jax.ShapeDtypeStruct((2, g, 2, n, n, lanes), F32)],
        scratch_shapes=[scratch, scratch, pltpu.VMEM((2, n, lanes), F32), scratch, scratch, scratch,
                        pltpu.VMEM((2, n, lanes), F32)],
        compiler_params=_cparams(3),
        name="rwkv_scan",
    )(r, k, v, w2, ag2, kk_p, ka_p, rk_p, s0)


def _rwkv_post_kernel(nb, tb_steps, of_ref, ob_ref, v_ref, bf_ref, bb_ref, lnw_ref, lnb_ref, y_ref):
    g = pl.program_id(1)
    n = RWKV_HEAD
    n_hp = v_ref.shape[1] // V7X_LANES

    def body(t, carry):
        row0 = pl.multiple_of(t * nb + g * V7X_SUBLANES, V7X_SUBLANES)
        o = of_ref[t] + ob_ref[t]
        mean = jnp.mean(o, axis=1, keepdims=True)
        var = jnp.mean(jnp.square(o - mean), axis=1, keepdims=True)
        on = (o - mean) * lax.rsqrt(var + GN_EPS)
        v = _lanes_from_rows(v_ref[pl.ds(row0, V7X_SUBLANES), :]).reshape(2, n, V7X_LANES)
        bonus = (bf_ref[t] + bb_ref[t])[:, None, :]
        y = on * lnw_ref[...] + lnb_ref[...] + bonus * v
        rows = y.reshape(2 * n, V7X_LANES).T
        for hp in range(n_hp):
            y_ref[pl.ds(row0, V7X_SUBLANES), hp * V7X_LANES:(hp + 1) * V7X_LANES] = (
                rows[hp * V7X_SUBLANES:(hp + 1) * V7X_SUBLANES, :])
        return carry

    lax.fori_loop(0, tb_steps, body, 0)


def _rwkv_post(seg, o2, v, bonus2, lnw_p, lnb_p):
    _, g, t, _, n, lanes = o2.shape
    d = v.shape[1]
    tb = min(SCAN_ROWS // seg.nb, t)
    blk = lambda e: pl.BlockSpec((None, None, tb, 2, n, lanes), lambda tt, gg: (e, gg, tt, 0, 0, 0))
    bon = lambda e: pl.BlockSpec((None, None, tb, 2, lanes), lambda tt, gg: (e, gg, tt, 0, 0))
    par = pl.BlockSpec((2, n, lanes), lambda tt, gg: (0, 0, 0))
    nat = pl.BlockSpec((tb * seg.nb, d), lambda tt, gg: (tt, 0))
    return pl.pallas_call(
        functools.partial(_rwkv_post_kernel, seg.nb, tb),
        grid=(t // tb, g),
        in_specs=[blk(0), blk(1), nat, bon(0), bon(1), par, par],
        out_specs=nat,
        out_shape=jax.ShapeDtypeStruct((seg.m, d), F32),
        compiler_params=_cparams(2),
        name="rwkv_post",
    )(o2, o2, v, bonus2, bonus2, lnw_p, lnb_p)


def _param_lanes(p):
    h = p.size // RWKV_HEAD
    q = p.reshape(h // 2, 2, RWKV_HEAD).transpose(1, 2, 0)
    return jnp.repeat(q, V7X_LANES // (h // 2), axis=2)


def _pad_lora_in(w):
    r = w.shape[-1]
    w = jnp.pad(w, ((0, 0), (0, 0), (0, LORA_PAD - r)))
    return jnp.concatenate([w[0], w[1]], axis=1)


def _pad_lora_out(w):
    r = w.shape[1]
    z = jnp.zeros((LORA_PAD, w.shape[2]), w.dtype)
    wp = jnp.pad(w, ((0, 0), (0, LORA_PAD - r), (0, 0)))
    return jnp.stack([jnp.concatenate([wp[0], z], axis=0), jnp.concatenate([z, wp[1]], axis=0)])


def _dir_mm(seg, x, w2, bias2, epi, name):
    k = x.shape[1]
    d = w2.shape[2]
    tm = _tile_rows(seg, TM_MM)
    n_i = seg.m // tm
    tn = 512
    return pl.pallas_call(
        functools.partial(_mm_kernel, 0, 1, None, epi),
        grid=(2 * n_i, d // tn),
        in_specs=[pl.BlockSpec((tm, k), lambda i, j: (i % n_i, 0)),
                  pl.BlockSpec((None, k, tn), lambda i, j: (i // n_i, 0, j)),
                  pl.BlockSpec((None, 1, tn), lambda i, j: (i // n_i, 0, j))],
        out_specs=pl.BlockSpec((None, tm, tn), lambda i, j: (i // n_i, i % n_i, j)),
        out_shape=jax.ShapeDtypeStruct((2, seg.m, d), F32),
        compiler_params=_cparams(2),
        name=name,
    )(x, w2, bias2.reshape(2, 1, d))


def _rwkv_mixer(seg, h, x, p, state, mod, l):
    d = h.shape[1]
    xr, xw, xk, xv, xa, xg = _shift_mix(seg, h, p['mu'])
    mmf = functools.partial(_mm, seg, tn=512)
    r = mmf(xr, p['w_r'], _w2_spec(d, 512), d, out_dtype=F32, name="rwkv_r")
    k = mmf(xk, p['w_k'], _w2_spec(d, 512), d, out_dtype=F32, name="rwkv_k")
    v = mmf(xv, p['w_v'], _w2_spec(d, 512), d, out_dtype=F32, name="rwkv_v")
    gl = _mm(seg, xg, p['g1'], _w2_spec(d, 256), 256, tn=256, out_dtype=BF16, name="rwkv_g1", epi=_epi_sigmoid)
    gate = mmf(gl, p['g2'], _w2_spec(256, 512), d, out_dtype=F32, name="rwkv_g2")
    nl = 2 * LORA_PAD
    wl = _mm(seg, xw, p['w1p'], _w2_spec(d, nl), nl, tn=nl, out_dtype=BF16, name="rwkv_w1", epi=_epi_tanh)
    al = _mm(seg, xa, p['a1p'], _w2_spec(d, nl), nl, tn=nl, out_dtype=BF16, name="rwkv_a1")
    dec2 = _dir_mm(seg, wl, p['w2p'], p['w0'], _epi_decay, "rwkv_w2")
    ag2 = _dir_mm(seg, al, p['a2p'], p['a0'], _epi_bias_sigmoid, "rwkv_a2")

    n_heads = d // RWKV_HEAD
    n = RWKV_HEAD
    g = seg.nb // V7X_SUBLANES
    if state is None:
        s0 = jnp.zeros((2, g, 2, n, n, V7X_LANES), F32)
    else:
        s0 = state.reshape(g, V7X_SUBLANES, 2, n_heads // 2, 2, n, n).transpose(2, 0, 4, 6, 5, 3, 1)
        s0 = s0.reshape(2, g, 2, n, n, V7X_LANES)
    o2, bonus2, sfin = _rwkv_scan(seg, r, k, v, dec2, ag2, p['kk_p'], p['ka_p'], p['rk_p'], s0)
    y = _rwkv_post(seg, o2, v, bonus2, p['lnw_p'], p['lnb_p'])
    x_new = _mm(seg, y, p['w_o'], _w2_spec(d, 512), d, tn=512, tm=TM_ROW, out_dtype=F32, name="rwkv_wo",
                pro=_pro_mul, pro_args=(('full', gate),),
                epi=_epi_resid, epi_args=(('tile', x), ('mod', mod, l, 2)))
    return x_new, sfin


def _head_rmsnorm(acc, w_ref):
    outs = []
    for hh in range(acc.shape[1] // ATTN_HEAD_DIM):
        xh = acc[:, hh * ATTN_HEAD_DIM:(hh + 1) * ATTN_HEAD_DIM]
        yh = xh * lax.rsqrt(jnp.mean(xh * xh, axis=-1, keepdims=True) + NORM_EPS)
        outs.append(yh * w_ref[...])
    return jnp.concatenate(outs, axis=1) if len(outs) > 1 else outs[0]


def _rope(y, cos_ref, sin_ref):
    nh = y.shape[1] // ATTN_HEAD_DIM
    cos = jnp.concatenate([cos_ref[...]] * nh, axis=1) if nh > 1 else cos_ref[...]
    sin = jnp.concatenate([sin_ref[...]] * nh, axis=1) if nh > 1 else sin_ref[...]
    width = y.shape[1]
    lane = lax.broadcasted_iota(jnp.int32, (1, width), 1)
    quarter = ATTN_HEAD_DIM // 4
    first = (lane & (2 * quarter - 1)) < quarter
    swapped = jnp.where(first, pltpu.roll(y, width - quarter, axis=1), pltpu.roll(y, quarter, axis=1))
    return y * cos + swapped * sin


def _epi_qk_rope(acc, nw_ref, cos_ref, sin_ref):
    return _rope(_head_rmsnorm(acc, nw_ref), cos_ref, sin_ref)


def _epi_qk(acc, nw_ref):
    return _head_rmsnorm(acc, nw_ref)


def _rope_tables(seg):
    t = seg.t
    row = (jnp.arange(t) // GRID_W).astype(F32)
    col = (jnp.arange(t) % GRID_W).astype(F32)
    n_freq = ATTN_HEAD_DIM // 4
    inv = ROPE_BASE ** (-jnp.arange(n_freq, dtype=F32) / n_freq)
    ar, ac = row[:, None] * inv, col[:, None] * inv
    cos = jnp.concatenate([jnp.cos(ar), jnp.cos(ar), jnp.cos(ac), jnp.cos(ac)], axis=1)
    sin = jnp.concatenate([-jnp.sin(ar), jnp.sin(ar), -jnp.sin(ac), jnp.sin(ac)], axis=1)
    return jnp.repeat(cos, seg.nb, axis=0), jnp.repeat(sin, seg.nb, axis=0)


def _attn_kernel(seq_nb, n_band, nblk, has_cache, *refs):
    n_tm = n_band if n_band else 1
    n_src = n_tm + (1 if has_cache else 0)
    q_refs = refs[:ATTN_GROUP]
    refs = refs[ATTN_GROUP:]
    k_refs = refs[:n_src]
    v_refs = refs[n_src:2 * n_src]
    sink_ref = refs[2 * n_src]
    o_refs = refs[2 * n_src + 1:]
    blk = pl.program_id(1)
    bq = q_refs[0].shape[0] // seq_nb

    def src(refs_, si, b):
        if si < n_tm:
            rows = refs_[si].shape[0] // seq_nb
            return refs_[si][pl.ds(b, rows, stride=seq_nb), :]
        return refs_[si][b]

    for b in range(seq_nb):
        q = jnp.concatenate([q_ref[pl.ds(b, bq, stride=seq_nb), :] for q_ref in q_refs], axis=0).astype(BF16)
        scores = []
        for si in range(n_src):
            s = lax.dot_general(q, src(k_refs, si, b).astype(BF16), (((1,), (1,)), ((), ())),
                                preferred_element_type=F32) * ATTN_SCALE
            if n_band == 3 and si in (0, 2):
                qi = lax.broadcasted_iota(jnp.int32, (ATTN_GROUP * bq, bq), 0) & (bq - 1)
                ki = lax.broadcasted_iota(jnp.int32, (ATTN_GROUP * bq, bq), 1)
                if si == 0:
                    valid, present = ki >= qi, blk > 0
                else:
                    valid, present = ki <= qi, blk < nblk - 1
                s = jnp.where(valid, s, NEG_INF) + jnp.where(present, 0.0, NEG_INF)
            scores.append(s)
        sink = sink_ref[...]

        def lane_tiles(a):
            return [a[:, c:c + V7X_LANES] for c in range(0, a.shape[1], V7X_LANES)]

        m = jnp.maximum(sink, jnp.max(functools.reduce(jnp.maximum, [c for s in scores for c in lane_tiles(s)]),
                                      axis=-1, keepdims=True))
        ps = [jnp.exp(s - m) for s in scores]
        den = jnp.exp(sink - m) + jnp.sum(functools.reduce(jnp.add, [c for p in ps for c in lane_tiles(p)]),
                                          axis=-1, keepdims=True)
        inv = 1.0 / den
        out = None
        for si in range(n_src):
            o = jnp.dot(ps[si].astype(BF16), src(v_refs, si, b).astype(BF16), preferred_element_type=F32)
            out = o if out is None else out + o
        out = out * inv
        for gq, o_ref in enumerate(o_refs):
            o_ref[pl.ds(b, bq, stride=seq_nb), :] = out[gq * bq:(gq + 1) * bq, :]


def _pro_heads(x0, *other_refs):
    parts = [x0] + [r[...] for r in other_refs]
    return jnp.concatenate([parts[gq][kv] for kv in range(x0.shape[0]) for gq in range(len(parts))], axis=1)


def _attn_mixer(seg, h, x, p, cache_k, cache_v, mod, l):
    d = h.shape[1]
    dh = ATTN_HEAD_DIM
    nkv = p['w_qkv'].shape[1] // dh // (ATTN_GROUP + 2)
    nq = nkv * ATTN_GROUP
    tn = 512
    norm_arg = lambda w: ('raw', w.reshape(1, dh), pl.BlockSpec((1, dh), lambda i, j: (0, 0)))
    if cache_k is None:
        epi, extra = _epi_qk, ()
    else:
        cos, sin = _rope_tables(seg)
        tm_q = _tile_rows(seg, TM_MM)
        tab = lambda a: ('raw', a, pl.BlockSpec((tm_q, dh), lambda i, j: (i, 0)))
        epi, extra = _epi_qk_rope, (tab(cos), tab(sin))
    q = _mm(seg, h, p['w_qkv'], _w2_spec(d, tn), nq * dh, tn=tn, out_dtype=F32, name="attn_q", epi=epi,
            epi_args=(norm_arg(p['q_norm']),) + extra, head_dim_out=dh)
    k = _mm(seg, h, p['w_qkv'], _w2_spec(d, tn, nq * dh // tn), nkv * dh, tn=tn, out_dtype=F32, name="attn_k",
            epi=epi, epi_args=(norm_arg(p['k_norm']),) + extra)
    v = _mm(seg, h, p['w_qkv'], _w2_spec(d, tn, (nq + nkv) * dh // tn), nkv * dh, tn=tn, out_dtype=F32,
            name="attn_v")

    bq = ATTN_BLOCK
    nblk = seg.t // bq
    rows = bq * seg.nb
    sink_col = jnp.repeat(p['sink'].reshape(nkv, ATTN_GROUP), bq, axis=1).reshape(nkv, ATTN_GROUP * bq, 1)
    sink_spec = pl.BlockSpec((None, ATTN_GROUP * bq, 1), lambda kv, qb: (kv, 0, 0))
    q_specs = [pl.BlockSpec((None, rows, dh), lambda kv, qb, gq=gq: (kv * ATTN_GROUP + gq, qb, 0))
               for gq in range(ATTN_GROUP)]
    q_args = (q,) * ATTN_GROUP
    if cache_k is None:
        full = pl.BlockSpec((seg.m, dh), lambda kv, qb: (0, kv))
        in_specs = [*q_specs, full, full, sink_spec]
        args = (*q_args, k, v, sink_col)
        n_band = 0
    else:
        past = cache_k.shape[1]
        ck = cache_k.reshape(seg.nb, past, nkv * dh)
        cv = cache_v.reshape(seg.nb, past, nkv * dh)

        def band(delta):
            return pl.BlockSpec((rows, dh), lambda kv, qb: (jnp.clip(qb + delta, 0, nblk - 1), kv))

        cache_spec = pl.BlockSpec((seg.nb, past, dh), lambda kv, qb: (0, 0, kv))
        in_specs = [*q_specs, band(-1), band(0), band(1), cache_spec, band(-1), band(0), band(1), cache_spec,
                    sink_spec]
        args = (*q_args, k, k, k, ck, v, v, v, cv, sink_col)
        n_band = 3
    o_parts = pl.pallas_call(
        functools.partial(_attn_kernel, seg.nb, n_band, nblk, cache_k is not None),
        grid=(nkv, nblk),
        in_specs=in_specs,
        out_specs=[pl.BlockSpec((None, rows, dh), lambda kv, qb: (kv, qb, 0))] * ATTN_GROUP,
        out_shape=[jax.ShapeDtypeStruct((nkv, seg.m, dh), F32)] * ATTN_GROUP,
        compiler_params=_cparams(2),
        name="attn_core",
    )(*args)

    tm_o = _tile_rows(seg, TM_MM)
    o_spec = pl.BlockSpec((nkv, tm_o, dh), lambda i, j: (0, i, 0))
    x_new = _mm(seg, o_parts[0], p['w_o'], _w2_spec(d, 512), d, tn=512, tm=TM_MM, out_dtype=F32, name="attn_wo",
                k=nq * dh, x_spec=o_spec, pro=_pro_heads, pro_args=tuple(('raw', a, o_spec) for a in o_parts[1:]),
                epi=_epi_resid, epi_args=(('tile', x), ('mod', mod, l, 2)))
    return x_new, k, v


def _ffn_up_kernel(nb, n_i, x_ref, xp_ref, xn_ref, wg_ref, wv_ref, cwg_ref, cwv_ref, cbg_ref, cbv_ref, o_ref):
    tm = x_ref.shape[0]
    halo = xp_ref.shape[0]
    xh = _with_halo(x_ref, xp_ref, xn_ref, n_i)

    def conv(w_ref, cw_ref, cb_ref):
        u = jnp.dot(xh, w_ref[...].astype(BF16), preferred_element_type=F32)
        prev = u[halo - nb:halo - nb + tm]
        cur = u[halo:halo + tm]
        nxt = u[halo + nb:halo + nb + tm]
        return prev * cw_ref[0:1, :] + cur * cw_ref[1:2, :] + nxt * cw_ref[2:3, :] + cb_ref[...]

    gate = conv(wg_ref, cwg_ref, cbg_ref)
    val = conv(wv_ref, cwv_ref, cbv_ref)
    o_ref[...] = (gate * jax.nn.sigmoid(gate) * val).astype(o_ref.dtype)


def _conv_ffn(seg, h, x, up, conv_w, conv_b, down, mod, l):
    d = h.shape[1]
    f = down.shape[1]
    tn = 512
    nj = f // tn
    tm = _tile_rows(seg, TM_FFN)
    n_i = seg.m // tm
    halo = max(V7X_BF16_ROWS, seg.nb)
    hp, hn = _halo_specs(tm, n_i, halo, d, lambda j: 0)
    cw = lambda off: pl.BlockSpec((None, conv_w.shape[1], tn), lambda i, j: (l, 0, j + off))
    cb = lambda off: pl.BlockSpec((None, 1, tn), lambda i, j: (l, 0, j + off))
    cb3 = conv_b.reshape(conv_b.shape[0], 1, 2 * f)
    act = pl.pallas_call(
        functools.partial(_ffn_up_kernel, seg.nb, n_i),
        grid=(n_i, nj),
        in_specs=[_full_spec(tm, d), hp, hn, _w3_spec(l, d, tn), _w3_spec(l, d, tn, nj), cw(0), cw(nj), cb(0), cb(nj)],
        out_specs=_tile_spec(tm, tn),
        out_shape=jax.ShapeDtypeStruct((seg.m, f), BF16),
        compiler_params=_cparams(2),
        name="ffn_up",
    )(h, h, h, up, up, conv_w, conv_w, cb3, cb3)
    return _mm(seg, act, down, _w3_spec(l, f, 256), d, tn=256, out_dtype=F32, name="ffn_down",
               epi=_epi_resid, epi_args=(('tile', x), ('mod', mod, l, 5)))


def kernel(x_prompt, x_sample, state_rwkv, cache_k, cache_v, c, c_ctx, ada_w, ada_b, norm_mix, norm_ffn, ffn_up, ffn_conv_w, ffn_conv_b, ffn_down, pool_w, pool_scale, rwkv_mu, rwkv_w_r, rwkv_w_k, rwkv_w_v, rwkv_w0, rwkv_w1, rwkv_w2, rwkv_a0, rwkv_a1, rwkv_a2, rwkv_g1, rwkv_g2, rwkv_k_k, rwkv_k_a, rwkv_r_k, rwkv_ln_w, rwkv_ln_b, rwkv_w_o, attn_w_qkv, attn_q_norm, attn_k_norm, attn_sink, attn_w_o):
    n_ctx, l_ctx, d = x_prompt.shape
    n_lat, l_lat, _ = x_sample.shape
    depth = ada_w.shape[0]
    assert n_ctx % V7X_SUBLANES == 0 and n_lat == V7X_SUBLANES, "8 latent sequences share one conditioning tile"
    assert d // RWKV_HEAD // 2 * V7X_SUBLANES == V7X_LANES, "8 sequences x head pairs must fill the lanes"

    n_cond = n_lat + 1
    ncp = -(-n_cond // V7X_SUBLANES) * V7X_SUBLANES
    cond = jnp.concatenate([c, c_ctx[None, :], jnp.zeros((ncp - n_cond, d), F32)], axis=0)
    mod = _adaln(cond, ada_w, ada_b)
    mod4 = mod.reshape(depth, ncp, 6, d).transpose(0, 2, 1, 3)
    mods = (mod4[:, :, n_lat:n_lat + 1], mod4[:, :, :n_lat])

    segs = (_Seg(n_ctx, l_ctx, 1), _Seg(n_lat, l_lat, n_lat))
    xs = [a.transpose(1, 0, 2).reshape(a.shape[0] * a.shape[1], d) for a in (x_prompt, x_sample)]

    n_mixers = 3
    slots = [0] * n_mixers
    new_state, new_k, new_v = [], [], []
    for l in range(depth):
        kind = l % n_mixers
        slot = slots[kind]
        slots[kind] += 1
        if kind == 1:
            p = {'mu': rwkv_mu[slot], 'w_r': rwkv_w_r[slot], 'w_k': rwkv_w_k[slot], 'w_v': rwkv_w_v[slot],
                 'w0': rwkv_w0[slot], 'w1p': _pad_lora_in(rwkv_w1[slot]), 'w2p': _pad_lora_out(rwkv_w2[slot]),
                 'a0': rwkv_a0[slot], 'a1p': _pad_lora_in(rwkv_a1[slot]), 'a2p': _pad_lora_out(rwkv_a2[slot]),
                 'g1': rwkv_g1[slot], 'g2': rwkv_g2[slot], 'w_o': rwkv_w_o[slot],
                 'kk_p': _param_lanes(rwkv_k_k[slot]), 'ka_p': _param_lanes(rwkv_k_a[slot]),
                 'rk_p': _param_lanes(rwkv_r_k[slot]), 'lnw_p': _param_lanes(rwkv_ln_w[slot]),
                 'lnb_p': _param_lanes(rwkv_ln_b[slot])}
        elif kind == 2:
            p = {'w_qkv': attn_w_qkv[slot], 'q_norm': attn_q_norm[slot], 'k_norm': attn_k_norm[slot],
                 'sink': attn_sink[slot], 'w_o': attn_w_o[slot]}
        for si, seg in enumerate(segs):
            mod_s = mods[si]
            h_dtype = BF16 if kind == 2 else F32
            x = xs[si]
            h = _norm_mod(seg, x, norm_mix, mod_s, l, 0, 1, h_dtype)
            if kind == 0:
                x = _pool_mixer(seg, h, x, pool_w[slot], pool_scale[slot], mod_s, l)
            elif kind == 1:
                x, sfin = _rwkv_mixer(seg, h, x, p, state_rwkv[:, slot] if si == 1 else None, mod_s, l)
                if si == 0:
                    g = seg.nb // V7X_SUBLANES
                    hp = d // RWKV_HEAD // 2
                    st = sfin.reshape(2, g, 2, RWKV_HEAD, RWKV_HEAD, hp, V7X_SUBLANES).transpose(1, 6, 0, 5, 2, 4, 3)
                    new_state.append(st.reshape(seg.nb, 2, 2 * hp, RWKV_HEAD, RWKV_HEAD))
            else:
                x, k, v = _attn_mixer(seg, h, x, p, cache_k[:, slot] if si == 1 else None,
                                      cache_v[:, slot] if si == 1 else None, mod_s, l)
                if si == 0:
                    nkv = k.shape[1] // ATTN_HEAD_DIM
                    to_bm = lambda a: a.reshape(seg.t, seg.nb, nkv, ATTN_HEAD_DIM).transpose(1, 0, 2, 3)
                    new_k.append(to_bm(k))
                    new_v.append(to_bm(v))
            h2 = _norm_mod(seg, x, norm_ffn, mod_s, l, 3, 4, BF16)
            xs[si] = _conv_ffn(seg, h2, x, ffn_up, ffn_conv_w, ffn_conv_b, ffn_down, mod_s, l)

    ys = [a.reshape(seg.t, seg.nb, d).transpose(1, 0, 2) for a, seg in zip(xs, segs)]
    return (ys[0], ys[1], jnp.stack(new_state, axis=1), jnp.stack(new_k, axis=1), jnp.stack(new_v, axis=1))
```

```python
import functools
from typing import NamedTuple

import jax
import jax.numpy as jnp
from jax import lax
from jax.experimental import pallas as pl
from jax.experimental.pallas import tpu as pltpu

F32 = jnp.float32
BF16 = jnp.bfloat16

V7X_LANES = 128
V7X_SUBLANES = 8
V7X_BF16_ROWS = 16
V7X_VMEM_LIMIT_BYTES = 56 * 1024 * 1024

NORM_EPS = 1e-6
GN_EPS = 64e-5
DECAY_SCALE = 0.606531
POOL_WINDOWS = (2, 4, 8, 16)
RWKV_HEAD = 64
ATTN_HEAD_DIM = 128
ATTN_GROUP = 4
ATTN_BLOCK = 128
GRID_W = 64
ROPE_BASE = 10000.0
ATTN_SCALE = ATTN_HEAD_DIM ** -0.5
NEG_INF = -1e30
LORA_PAD = 128

TM_FFN = 2048
TM_MM = 1024
TM_ROW = 512
SCAN_ROWS = 256


class _Seg(NamedTuple):
    nb: int
    t: int
    mod_rows: int

    @property
    def m(self):
        return self.nb * self.t


def _cparams(n_axes):
    return pltpu.CompilerParams(dimension_semantics=("arbitrary",) * n_axes,
                                vmem_limit_bytes=V7X_VMEM_LIMIT_BYTES)


def _tile_rows(seg, target):
    return min(target, seg.m)


def _full_spec(tm, w):
    return pl.BlockSpec((tm, w), lambda i, j: (i, 0))


def _tile_spec(tm, tn):
    return pl.BlockSpec((tm, tn), lambda i, j: (i, j))


def _mod_spec(seg, l, which, tn):
    return pl.BlockSpec((None, None, seg.mod_rows, tn), lambda i, j: (l, which, 0, j))


def _per_row(fn, y, m):
    if m.shape[0] == 1:
        return fn(y, m)
    tm, w = y.shape
    return fn(y.reshape(tm // m.shape[0], m.shape[0], w), m[None]).reshape(tm, w)


def _halo_specs(tm, n_i, rows, width_block, col_fn):
    per = tm // rows
    n_blocks = n_i * per
    prev = pl.BlockSpec((rows, width_block), lambda i, j: (jnp.maximum(i * per - 1, 0), col_fn(j)))
    nxt = pl.BlockSpec((rows, width_block), lambda i, j: (jnp.minimum((i + 1) * per, n_blocks - 1), col_fn(j)))
    return prev, nxt


def _with_halo(x_ref, prev_ref, next_ref, n_i):
    i = pl.program_id(0)
    prev = jnp.where(i > 0, prev_ref[...], jnp.zeros_like(prev_ref[...]))
    nxt = jnp.where(i < n_i - 1, next_ref[...], jnp.zeros_like(next_ref[...]))
    return jnp.concatenate([prev, x_ref[...], nxt], axis=0)


def _adaln_kernel(c_ref, w_ref, b_ref, o_ref):
    c = c_ref[...]
    s = c * jax.nn.sigmoid(c)
    acc = jnp.dot(s.astype(BF16), w_ref[...].astype(BF16), preferred_element_type=F32)
    o_ref[...] = acc + b_ref[...]


def _adaln(cond, ada_w, ada_b):
    depth, d, n6 = ada_w.shape
    ncp = cond.shape[0]
    tn = 1024
    return pl.pallas_call(
        _adaln_kernel,
        grid=(depth, n6 // tn),
        in_specs=[pl.BlockSpec((ncp, d), lambda l, j: (0, 0)),
                  pl.BlockSpec((None, d, tn), lambda l, j: (l, 0, j)),
                  pl.BlockSpec((None, 1, tn), lambda l, j: (l, 0, j))],
        out_specs=pl.BlockSpec((None, ncp, tn), lambda l, j: (l, 0, j)),
        out_shape=jax.ShapeDtypeStruct((depth, ncp, n6), F32),
        compiler_params=_cparams(2),
        name="adaln",
    )(cond, ada_w, ada_b.reshape(depth, 1, n6))


def _norm_mod_math(x, g, sh, sc):
    y = x * lax.rsqrt(jnp.mean(x * x, axis=-1, keepdims=True) + NORM_EPS)
    y = y * g
    return _per_row(lambda a, b: a * (1.0 + b), y, sc), sh


def _norm_mod_kernel(x_ref, g_ref, sh_ref, sc_ref, o_ref):
    y, sh = _norm_mod_math(x_ref[...], g_ref[...], sh_ref[...], sc_ref[...])
    o_ref[...] = _per_row(lambda a, b: a + b, y, sh).astype(o_ref.dtype)


def _norm_mod(seg, x, g, mod, l, shift_i, scale_i, out_dtype):
    d = g.shape[-1]
    tm = _tile_rows(seg, TM_ROW)
    return pl.pallas_call(
        _norm_mod_kernel,
        grid=(seg.m // tm, 1),
        in_specs=[_full_spec(tm, d), pl.BlockSpec((None, 1, d), lambda i, j: (l, 0, 0)),
                  _mod_spec(seg, l, shift_i, d), _mod_spec(seg, l, scale_i, d)],
        out_specs=_full_spec(tm, d),
        out_shape=jax.ShapeDtypeStruct((seg.m, d), out_dtype),
        compiler_params=_cparams(2),
        name="norm_mod",
    )(x, g.reshape(g.shape[0], 1, d), mod, mod)


def _mm_kernel(n_pro, n_epi, pro, epi, x_ref, *refs):
    pro_refs = refs[:n_pro]
    w_ref = refs[n_pro]
    epi_refs = refs[n_pro + 1:n_pro + 1 + n_epi]
    o_ref = refs[n_pro + 1 + n_epi]
    x = x_ref[...]
    if pro is not None:
        x = pro(x, *pro_refs)
    acc = jnp.dot(x.astype(BF16), w_ref[...].astype(BF16), preferred_element_type=F32)
    if epi is not None:
        acc = epi(acc, *epi_refs)
    if len(o_ref.shape) == 3:
        hd = o_ref.shape[2]
        for hh in range(o_ref.shape[0]):
            o_ref[hh] = acc[:, hh * hd:(hh + 1) * hd].astype(o_ref.dtype)
    else:
        o_ref[...] = acc.astype(o_ref.dtype)


def _mm(seg, x, w, w_spec, n, *, tn, out_dtype, name, tm=None, k=None, x_spec=None, pro=None, pro_args=(),
        epi=None, epi_args=(), head_dim_out=None):
    k = x.shape[1] if k is None else k
    tm = _tile_rows(seg, TM_MM if tm is None else tm)

    def lower(arg):
        kind = arg[0]
        if kind == 'full':
            return arg[1], _full_spec(tm, arg[1].shape[1])
        if kind == 'tile':
            return arg[1], _tile_spec(tm, tn)
        if kind == 'row':
            return arg[1], pl.BlockSpec((1, tn), lambda i, j: (0, j))
        if kind == 'mod':
            return arg[1], _mod_spec(seg, arg[2], arg[3], tn)
        return arg[1], arg[2]

    pro_l = [lower(a) for a in pro_args]
    epi_l = [lower(a) for a in epi_args]
    if head_dim_out is None:
        out_spec = _tile_spec(tm, tn)
        out_shape = jax.ShapeDtypeStruct((seg.m, n), out_dtype)
    else:
        out_spec = pl.BlockSpec((tn // head_dim_out, tm, head_dim_out), lambda i, j: (j, i, 0))
        out_shape = jax.ShapeDtypeStruct((n // head_dim_out, seg.m, head_dim_out), out_dtype)
    return pl.pallas_call(
        functools.partial(_mm_kernel, len(pro_l), len(epi_l), pro, epi),
        grid=(seg.m // tm, n // tn),
        in_specs=[_full_spec(tm, k) if x_spec is None else x_spec, *[sp for _, sp in pro_l], w_spec,
                  *[sp for _, sp in epi_l]],
        out_specs=out_spec,
        out_shape=out_shape,
        compiler_params=_cparams(2),
        name=name,
    )(x, *[a for a, _ in pro_l], w, *[a for a, _ in epi_l])


def _w2_spec(k, tn, joff=0):
    return pl.BlockSpec((k, tn), lambda i, j: (0, j + joff))


def _w3_spec(l, k, tn, joff=0):
    return pl.BlockSpec((None, k, tn), lambda i, j: (l, 0, j + joff))


def _epi_resid(acc, x_ref, gate_ref):
    return x_ref[...] + _per_row(lambda a, b: a * b, acc, gate_ref[...])


def _epi_sigmoid(acc):
    return jax.nn.sigmoid(acc)


def _epi_tanh(acc):
    return jnp.tanh(acc)


def _epi_decay(acc, w0_ref):
    return jnp.exp(-DECAY_SCALE * jax.nn.sigmoid(w0_ref[...] + acc))


def _epi_bias_sigmoid(acc, a0_ref):
    return jax.nn.sigmoid(a0_ref[...] + acc)


def _pro_mul(x, g_ref):
    return x * g_ref[...]


def _pool_kernel(seg, n_i, h_ref, hp_ref, hn_ref, x_ref, w_ref, sc_ref, gate_ref, o_ref):
    i = pl.program_id(0)
    g = pl.program_id(1)
    tm = h_ref.shape[0]
    nb = seg.nb
    halo = hp_ref.shape[0]
    steps = tm // nb
    t_abs = i * steps + lax.broadcasted_iota(jnp.int32, (tm, 1), 0) // nb

    for gi, win in enumerate(POOL_WINDOWS):
        @pl.when(g == gi)
        def _(win=win):
            left = win // 2
            right = win - 1 - left
            ext = _with_halo(h_ref, hp_ref, hn_ref, n_i)
            acc, width = ext, 1
            while width < win:
                shift = width * nb
                acc = acc[shift:] + acc[:acc.shape[0] - shift]
                width *= 2
            start = halo - left * nb
            total = acc[start:start + tm]
            cnt = (jnp.minimum(t_abs + right + 1, seg.t) - jnp.maximum(t_abs - left, 0)).astype(F32)
            h = h_ref[...]
            pooled = total / cnt - h
            y = jnp.dot(pooled.astype(BF16), w_ref[...].astype(BF16), preferred_element_type=F32) * sc_ref[...]
            o_ref[...] = x_ref[...] + _per_row(lambda a, b: a * b, y, gate_ref[...])


def _pool_mixer(seg, h, x, w_grp, scale, mod, l):
    ngrp, gsz, _ = w_grp.shape
    d = h.shape[1]
    tm = _tile_rows(seg, TM_MM)
    n_i = seg.m // tm
    halo = max(POOL_WINDOWS) // 2 * seg.nb
    hp, hn = _halo_specs(tm, n_i, halo, gsz, lambda j: j)
    return pl.pallas_call(
        functools.partial(_pool_kernel, seg, n_i),
        grid=(n_i, ngrp),
        in_specs=[_tile_spec(tm, gsz), hp, hn, _tile_spec(tm, gsz),
                  pl.BlockSpec((None, gsz, gsz), lambda i, j: (j, 0, 0)),
                  pl.BlockSpec((1, gsz), lambda i, j: (0, j)),
                  _mod_spec(seg, l, 2, gsz)],
        out_specs=_tile_spec(tm, gsz),
        out_shape=jax.ShapeDtypeStruct((seg.m, d), F32),
        compiler_params=_cparams(2),
        name="pool_mixer",
    )(h, h, h, x, w_grp, scale.reshape(1, d), mod)


def _shift_mix_kernel(nb, n_i, h_ref, hp_ref, hn_ref, mu_ref, *o_refs):
    tm = h_ref.shape[0]
    ext = _with_halo(h_ref, hp_ref, hn_ref, n_i)
    h = h_ref[...]
    xx = 0.5 * (ext[0:tm] + ext[2 * nb:2 * nb + tm]) - h
    for n, o_ref in enumerate(o_refs):
        o_ref[...] = (h + xx * mu_ref[n:n + 1, :]).astype(o_ref.dtype)


def _shift_mix(seg, h, mu):
    d = h.shape[1]
    tn = 512
    n_out = mu.shape[0]
    tm = _tile_rows(seg, TM_MM)
    n_i = seg.m // tm
    hp, hn = _halo_specs(tm, n_i, seg.nb, tn, lambda j: j)
    return pl.pallas_call(
        functools.partial(_shift_mix_kernel, seg.nb, n_i),
        grid=(n_i, d // tn),
        in_specs=[_tile_spec(tm, tn), hp, hn, pl.BlockSpec((n_out, tn), lambda i, j: (0, j))],
        out_specs=[_tile_spec(tm, tn)] * n_out,
        out_shape=[jax.ShapeDtypeStruct((seg.m, d), BF16)] * n_out,
        compiler_params=_cparams(2),
        name="rwkv_shift_mix",
    )(h, h, h, mu)


def _lanes_from_rows(rows):
    n_hp = rows.shape[1] // V7X_LANES
    stacked = jnp.concatenate([rows[:, hp * V7X_LANES:(hp + 1) * V7X_LANES] for hp in range(n_hp)], axis=0)
    return stacked.T


def _scan_kernel(nb, tb_steps, r_ref, k_ref, v_ref, w_ref, ag_ref, kk_ref, ka_ref, rk_ref, s0_ref,
                 o_ref, bonus_ref, s_ref, r_s, v_s, w_s, a_s, b_s, kd_s, p_last):
    e = pl.program_id(0)
    g = pl.program_id(1)
    tb = pl.program_id(2)
    n = RWKV_HEAD
    nvb = n // V7X_SUBLANES

    @pl.when(tb == 0)
    def _():
        s_ref[...] = s0_ref[...]

    def load(ref, t):
        row0 = pl.multiple_of(t * nb + g * V7X_SUBLANES, V7X_SUBLANES)
        return _lanes_from_rows(ref[pl.ds(row0, V7X_SUBLANES), :]).reshape(2, n, V7X_LANES)

    def time_of(s):
        return jnp.where(e == 0, s, tb_steps - 1 - s)

    def prep(s, p_prev):
        live = s < tb_steps
        t = time_of(jnp.minimum(s, tb_steps - 1))
        p_prev = jnp.where(live, p_prev, p_last[...])
        p_last[...] = p_prev
        k = load(k_ref, t)
        ag = load(ag_ref, t)
        r = load(r_ref, t)
        kk = k * kk_ref[...]
        kk = kk / jnp.maximum(jnp.sqrt(jnp.sum(kk * kk, axis=1, keepdims=True)), 1e-12)
        kd = k * (1.0 + (ag - 1.0) * ka_ref[...])
        p_new = p_prev * load(w_ref, t)
        inv = 1.0 / p_new
        r_s[t] = r * p_new
        v_s[t] = load(v_ref, t)
        a_s[t] = -kk * p_prev
        b_s[t] = kk * ag * inv
        kd_s[t] = kd * inv
        bonus_ref[t] = jnp.sum(r * kd * rk_ref[...], axis=1)
        return p_new

    def vslice(vb):
        return pl.ds(vb * V7X_SUBLANES, V7X_SUBLANES)

    zeros = tuple(jnp.zeros((V7X_SUBLANES, V7X_LANES), F32) for _ in range(nvb))

    def step(s, p):
        p = prep(s + 1, p)
        t = time_of(s)
        for par in range(2):
            def pass1(kb, acc):
                acc = list(acc)
                for kr in range(V7X_SUBLANES):
                    ki = kb * V7X_SUBLANES + kr
                    a_row = a_s[t, par, pl.ds(ki, 1), :]
                    for vb in range(nvb):
                        acc[vb] = acc[vb] + s_ref[par, ki, vslice(vb), :] * a_row
                return tuple(acc)

            sa = lax.fori_loop(0, n // V7X_SUBLANES, pass1, zeros, unroll=True)
            vv = [v_s[t, par, vslice(vb), :] for vb in range(nvb)]

            def pass2(kb, acc):
                acc = list(acc)
                for kr in range(V7X_SUBLANES):
                    ki = kb * V7X_SUBLANES + kr
                    b_row = b_s[t, par, pl.ds(ki, 1), :]
                    k_row = kd_s[t, par, pl.ds(ki, 1), :]
                    r_row = r_s[t, par, pl.ds(ki, 1), :]
                    for vb in range(nvb):
                        sn = s_ref[par, ki, vslice(vb), :] + sa[vb] * b_row + vv[vb] * k_row
                        s_ref[par, ki, vslice(vb), :] = sn
                        acc[vb] = acc[vb] + sn * r_row
                return tuple(acc)

            o = lax.fori_loop(0, n // V7X_SUBLANES, pass2, zeros, unroll=True)
            for vb in range(nvb):
                o_ref[t, par, vslice(vb), :] = o[vb]
        return p

    p0 = prep(0, jnp.ones((2, n, V7X_LANES), F32))
    w_s[...] = lax.fori_loop(0, tb_steps, step, p0)

    def rescale(ki, carry):
        for par in range(2):
            s_ref[par, ki] = s_ref[par, ki] * w_s[par, pl.ds(ki, 1), :]
        return carry

    lax.fori_loop(0, n, rescale, 0)


def _rwkv_scan(seg, r, k, v, w2, ag2, kk_p, ka_p, rk_p, s0):
    n, lanes = RWKV_HEAD, V7X_LANES
    d = r.shape[1]
    t = seg.t
    g = seg.nb // V7X_SUBLANES
    tb = min(SCAN_ROWS // seg.nb, t)
    nt = t // tb

    def tmap(e, tt):
        return jnp.where(e == 0, tt, nt - 1 - tt)

    shared = pl.BlockSpec((tb * seg.nb, d), lambda e, gg, tt: (tmap(e, tt), 0))
    per_dir = pl.BlockSpec((None, tb * seg.nb, d), lambda e, gg, tt: (e, tmap(e, tt), 0))
    par = pl.BlockSpec((2, n, lanes), lambda e, gg, tt: (0, 0, 0))
    st = pl.BlockSpec((None, None, 2, n, n, lanes), lambda e, gg, tt: (e, gg, 0, 0, 0, 0))
    o_spec = pl.BlockSpec((None, None, tb, 2, n, lanes), lambda e, gg, tt: (e, gg, tmap(e, tt), 0, 0, 0))
    bon_spec = pl.BlockSpec((None, None, tb, 2, lanes), lambda e, gg, tt: (e, gg, tmap(e, tt), 0, 0))
    scratch = pltpu.VMEM((tb, 2, n, lanes), F32)
    return pl.pallas_call(
        functools.partial(_scan_kernel, seg.nb, tb),
        grid=(2, g, nt),
        in_specs=[shared, shared, shared, per_dir, per_dir, par, par, par, st],
        out_specs=[o_spec, bon_spec, st],
        out_shape=[jax.ShapeDtypeStruct((2, g, t, 2, n, lanes), F32),
                   jax.ShapeDtypeStruct((2, g, t, 2, lanes), F32),
                   jax.ShapeDtypeStruct((2, g, 2, n, n, lanes), F32)],
        scratch_shapes=[scratch, scratch, pltpu.VMEM((2, n, lanes), F32), scratch, scratch, scratch,
                        pltpu.VMEM((2, n, lanes), F32)],
        compiler_params=_cparams(3),
        name="rwkv_scan",
    )(r, k, v, w2, ag2, kk_p, ka_p, rk_p, s0)


def _rwkv_post_kernel(nb, tb_steps, of_ref, ob_ref, v_ref, bf_ref, bb_ref, lnw_ref, lnb_ref, y_ref):
    g = pl.program_id(1)
    n = RWKV_HEAD
    n_hp = v_ref.shape[1] // V7X_LANES

    def body(t, carry):
        row0 = pl.multiple_of(t * nb + g * V7X_SUBLANES, V7X_SUBLANES)
        o = of_ref[t] + ob_ref[t]
        mean = jnp.mean(o, axis=1, keepdims=True)
        var = jnp.mean(jnp.square(o - mean), axis=1, keepdims=True)
        on = (o - mean) * lax.rsqrt(var + GN_EPS)
        v = _lanes_from_rows(v_ref[pl.ds(row0, V7X_SUBLANES), :]).reshape(2, n, V7X_LANES)
        bonus = (bf_ref[t] + bb_ref[t])[:, None, :]
        y = on * lnw_ref[...] + lnb_ref[...] + bonus * v
        rows = y.reshape(2 * n, V7X_LANES).T
        for hp in range(n_hp):
            y_ref[pl.ds(row0, V7X_SUBLANES), hp * V7X_LANES:(hp + 1) * V7X_LANES] = (
                rows[hp * V7X_SUBLANES:(hp + 1) * V7X_SUBLANES, :])
        return carry

    lax.fori_loop(0, tb_steps, body, 0)


def _rwkv_post(seg, o2, v, bonus2, lnw_p, lnb_p):
    _, g, t, _, n, lanes = o2.shape
    d = v.shape[1]
    tb = min(SCAN_ROWS // seg.nb, t)
    blk = lambda e: pl.BlockSpec((None, None, tb, 2, n, lanes), lambda tt, gg: (e, gg, tt, 0, 0, 0))
    bon = lambda e: pl.BlockSpec((None, None, tb, 2, lanes), lambda tt, gg: (e, gg, tt, 0, 0))
    par = pl.BlockSpec((2, n, lanes), lambda tt, gg: (0, 0, 0))
    nat = pl.BlockSpec((tb * seg.nb, d), lambda tt, gg: (tt, 0))
    return pl.pallas_call(
        functools.partial(_rwkv_post_kernel, seg.nb, tb),
        grid=(t // tb, g),
        in_specs=[blk(0), blk(1), nat, bon(0), bon(1), par, par],
        out_specs=nat,
        out_shape=jax.ShapeDtypeStruct((seg.m, d), F32),
        compiler_params=_cparams(2),
        name="rwkv_post",
    )(o2, o2, v, bonus2, bonus2, lnw_p, lnb_p)


def _param_lanes(p):
    h = p.size // RWKV_HEAD
    q = p.reshape(h // 2, 2, RWKV_HEAD).transpose(1, 2, 0)
    return jnp.repeat(q, V7X_LANES // (h // 2), axis=2)


def _pad_lora_in(w):
    r = w.shape[-1]
    w = jnp.pad(w, ((0, 0), (0, 0), (0, LORA_PAD - r)))
    return jnp.concatenate([w[0], w[1]], axis=1)


def _pad_lora_out(w):
    r = w.shape[1]
    z = jnp.zeros((LORA_PAD, w.shape[2]), w.dtype)
    wp = jnp.pad(w, ((0, 0), (0, LORA_PAD - r), (0, 0)))
    return jnp.stack([jnp.concatenate([wp[0], z], axis=0), jnp.concatenate([z, wp[1]], axis=0)])


def _dir_mm(seg, x, w2, bias2, epi, name):
    k = x.shape[1]
    d = w2.shape[2]
    tm = _tile_rows(seg, TM_FFN)
    n_i = seg.m // tm
    tn = 512
    return pl.pallas_call(
        functools.partial(_mm_kernel, 0, 1, None, epi),
        grid=(2 * n_i, d // tn),
        in_specs=[pl.BlockSpec((tm, k), lambda i, j: (i % n_i, 0)),
                  pl.BlockSpec((None, k, tn), lambda i, j: (i // n_i, 0, j)),
                  pl.BlockSpec((None, 1, tn), lambda i, j: (i // n_i, 0, j))],
        out_specs=pl.BlockSpec((None, tm, tn), lambda i, j: (i // n_i, i % n_i, j)),
        out_shape=jax.ShapeDtypeStruct((2, seg.m, d), F32),
        compiler_params=_cparams(2),
        name=name,
    )(x, w2, bias2.reshape(2, 1, d))


def _rwkv_mixer(seg, h, x, p, state, mod, l):
    d = h.shape[1]
    xr, xw, xk, xv, xa, xg = _shift_mix(seg, h, p['mu'])
    mmf = functools.partial(_mm, seg, tn=512, tm=TM_FFN)
    r = mmf(xr, p['w_r'], _w2_spec(d, 512), d, out_dtype=F32, name="rwkv_r")
    k = mmf(xk, p['w_k'], _w2_spec(d, 512), d, out_dtype=F32, name="rwkv_k")
    v = mmf(xv, p['w_v'], _w2_spec(d, 512), d, out_dtype=F32, name="rwkv_v")
    gl = _mm(seg, xg, p['g1'], _w2_spec(d, 256), 256, tn=256, out_dtype=BF16, name="rwkv_g1", epi=_epi_sigmoid)
    gate = mmf(gl, p['g2'], _w2_spec(256, 512), d, out_dtype=F32, name="rwkv_g2")
    nl = 2 * LORA_PAD
    wl = _mm(seg, xw, p['w1p'], _w2_spec(d, nl), nl, tn=nl, out_dtype=BF16, name="rwkv_w1", epi=_epi_tanh)
    al = _mm(seg, xa, p['a1p'], _w2_spec(d, nl), nl, tn=nl, out_dtype=BF16, name="rwkv_a1")
    dec2 = _dir_mm(seg, wl, p['w2p'], p['w0'], _epi_decay, "rwkv_w2")
    ag2 = _dir_mm(seg, al, p['a2p'], p['a0'], _epi_bias_sigmoid, "rwkv_a2")

    n_heads = d // RWKV_HEAD
    n = RWKV_HEAD
    g = seg.nb // V7X_SUBLANES
    if state is None:
        s0 = jnp.zeros((2, g, 2, n, n, V7X_LANES), F32)
    else:
        s0 = state.reshape(g, V7X_SUBLANES, 2, n_heads // 2, 2, n, n).transpose(2, 0, 4, 6, 5, 3, 1)
        s0 = s0.reshape(2, g, 2, n, n, V7X_LANES)
    o2, bonus2, sfin = _rwkv_scan(seg, r, k, v, dec2, ag2, p['kk_p'], p['ka_p'], p['rk_p'], s0)
    y = _rwkv_post(seg, o2, v, bonus2, p['lnw_p'], p['lnb_p'])
    x_new = _mm(seg, y, p['w_o'], _w2_spec(d, 512), d, tn=512, tm=TM_ROW, out_dtype=F32, name="rwkv_wo",
                pro=_pro_mul, pro_args=(('full', gate),),
                epi=_epi_resid, epi_args=(('tile', x), ('mod', mod, l, 2)))
    return x_new, sfin


def _head_rmsnorm(acc, w_ref):
    outs = []
    for hh in range(acc.shape[1] // ATTN_HEAD_DIM):
        xh = acc[:, hh * ATTN_HEAD_DIM:(hh + 1) * ATTN_HEAD_DIM]
        yh = xh * lax.rsqrt(jnp.mean(xh * xh, axis=-1, keepdims=True) + NORM_EPS)
        outs.append(yh * w_ref[...])
    return jnp.concatenate(outs, axis=1) if len(outs) > 1 else outs[0]


def _rope(y, cos_ref, sin_ref):
    nh = y.shape[1] // ATTN_HEAD_DIM
    cos = jnp.concatenate([cos_ref[...]] * nh, axis=1) if nh > 1 else cos_ref[...]
    sin = jnp.concatenate([sin_ref[...]] * nh, axis=1) if nh > 1 else sin_ref[...]
    width = y.shape[1]
    lane = lax.broadcasted_iota(jnp.int32, (1, width), 1)
    quarter = ATTN_HEAD_DIM // 4
    first = (lane & (2 * quarter - 1)) < quarter
    swapped = jnp.where(first, pltpu.roll(y, width - quarter, axis=1), pltpu.roll(y, quarter, axis=1))
    return y * cos + swapped * sin


def _epi_qk_rope(acc, nw_ref, cos_ref, sin_ref):
    return _rope(_head_rmsnorm(acc, nw_ref), cos_ref, sin_ref)


def _epi_qk(acc, nw_ref):
    return _head_rmsnorm(acc, nw_ref)


def _rope_tables(seg):
    t = seg.t
    row = (jnp.arange(t) // GRID_W).astype(F32)
    col = (jnp.arange(t) % GRID_W).astype(F32)
    n_freq = ATTN_HEAD_DIM // 4
    inv = ROPE_BASE ** (-jnp.arange(n_freq, dtype=F32) / n_freq)
    ar, ac = row[:, None] * inv, col[:, None] * inv
    cos = jnp.concatenate([jnp.cos(ar), jnp.cos(ar), jnp.cos(ac), jnp.cos(ac)], axis=1)
    sin = jnp.concatenate([-jnp.sin(ar), jnp.sin(ar), -jnp.sin(ac), jnp.sin(ac)], axis=1)
    return jnp.repeat(cos, seg.nb, axis=0), jnp.repeat(sin, seg.nb, axis=0)


def _attn_kernel(seq_nb, n_band, nblk, has_cache, *refs):
    n_tm = n_band if n_band else 1
    n_src = n_tm + (1 if has_cache else 0)
    q_refs = refs[:ATTN_GROUP]
    refs = refs[ATTN_GROUP:]
    k_refs = refs[:n_src]
    v_refs = refs[n_src:2 * n_src]
    sink_ref = refs[2 * n_src]
    o_refs = refs[2 * n_src + 1:]
    blk = pl.program_id(1)
    bq = q_refs[0].shape[0] // seq_nb

    def src(refs_, si, b):
        if si < n_tm:
            rows = refs_[si].shape[0] // seq_nb
            return refs_[si][pl.ds(b, rows, stride=seq_nb), :]
        return refs_[si][b]

    for b in range(seq_nb):
        q = jnp.concatenate([q_ref[pl.ds(b, bq, stride=seq_nb), :] for q_ref in q_refs], axis=0).astype(BF16)
        scores = []
        for si in range(n_src):
            s = lax.dot_general(q, src(k_refs, si, b).astype(BF16), (((1,), (1,)), ((), ())),
                                preferred_element_type=F32) * ATTN_SCALE
            if n_band == 3 and si in (0, 2):
                qi = lax.broadcasted_iota(jnp.int32, (ATTN_GROUP * bq, bq), 0) & (bq - 1)
                ki = lax.broadcasted_iota(jnp.int32, (ATTN_GROUP * bq, bq), 1)
                if si == 0:
                    valid, present = ki >= qi, blk > 0
                else:
                    valid, present = ki <= qi, blk < nblk - 1
                s = jnp.where(valid, s, NEG_INF) + jnp.where(present, 0.0, NEG_INF)
            scores.append(s)
        sink = sink_ref[...]

        def lane_tiles(a):
            return [a[:, c:c + V7X_LANES] for c in range(0, a.shape[1], V7X_LANES)]

        m = jnp.maximum(sink, jnp.max(functools.reduce(jnp.maximum, [c for s in scores for c in lane_tiles(s)]),
                                      axis=-1, keepdims=True))
        ps = [jnp.exp(s - m) for s in scores]
        den = jnp.exp(sink - m) + jnp.sum(functools.reduce(jnp.add, [c for p in ps for c in lane_tiles(p)]),
                                          axis=-1, keepdims=True)
        inv = 1.0 / den
        out = None
        for si in range(n_src):
            o = jnp.dot(ps[si].astype(BF16), src(v_refs, si, b).astype(BF16), preferred_element_type=F32)
            out = o if out is None else out + o
        out = out * inv
        for gq, o_ref in enumerate(o_refs):
            o_ref[pl.ds(b, bq, stride=seq_nb), :] = out[gq * bq:(gq + 1) * bq, :]


def _pro_heads(x0, *other_refs):
    parts = [x0] + [r[...] for r in other_refs]
    return jnp.concatenate([parts[gq][kv] for kv in range(x0.shape[0]) for gq in range(len(parts))], axis=1)


def _attn_mixer(seg, h, x, p, cache_k, cache_v, mod, l):
    d = h.shape[1]
    dh = ATTN_HEAD_DIM
    nkv = p['w_qkv'].shape[1] // dh // (ATTN_GROUP + 2)
    nq = nkv * ATTN_GROUP
    tn = 512
    norm_arg = lambda w: ('raw', w.reshape(1, dh), pl.BlockSpec((1, dh), lambda i, j: (0, 0)))
    if cache_k is None:
        epi, extra = _epi_qk, ()
    else:
        cos, sin = _rope_tables(seg)
        tm_q = _tile_rows(seg, TM_MM)
        tab = lambda a: ('raw', a, pl.BlockSpec((tm_q, dh), lambda i, j: (i, 0)))
        epi, extra = _epi_qk_rope, (tab(cos), tab(sin))
    q = _mm(seg, h, p['w_qkv'], _w2_spec(d, tn), nq * dh, tn=tn, out_dtype=F32, name="attn_q", epi=epi,
            epi_args=(norm_arg(p['q_norm']),) + extra, head_dim_out=dh)
    k = _mm(seg, h, p['w_qkv'], _w2_spec(d, tn, nq * dh // tn), nkv * dh, tn=tn, out_dtype=F32, name="attn_k",
            epi=epi, epi_args=(norm_arg(p['k_norm']),) + extra)
    v = _mm(seg, h, p['w_qkv'], _w2_spec(d, tn, (nq + nkv) * dh // tn), nkv * dh, tn=tn, out_dtype=F32,
            name="attn_v")

    bq = ATTN_BLOCK
    nblk = seg.t // bq
    rows = bq * seg.nb
    sink_col = jnp.repeat(p['sink'].reshape(nkv, ATTN_GROUP), bq, axis=1).reshape(nkv, ATTN_GROUP * bq, 1)
    sink_spec = pl.BlockSpec((None, ATTN_GROUP * bq, 1), lambda kv, qb: (kv, 0, 0))
    q_specs = [pl.BlockSpec((None, rows, dh), lambda kv, qb, gq=gq: (kv * ATTN_GROUP + gq, qb, 0))
               for gq in range(ATTN_GROUP)]
    q_args = (q,) * ATTN_GROUP
    if cache_k is None:
        full = pl.BlockSpec((seg.m, dh), lambda kv, qb: (0, kv))
        in_specs = [*q_specs, full, full, sink_spec]
        args = (*q_args, k, v, sink_col)
        n_band = 0
    else:
        past = cache_k.shape[1]
        ck = cache_k.reshape(seg.nb, past, nkv * dh)
        cv = cache_v.reshape(seg.nb, past, nkv * dh)

        def band(delta):
            return pl.BlockSpec((rows, dh), lambda kv, qb: (jnp.clip(qb + delta, 0, nblk - 1), kv))

        cache_spec = pl.BlockSpec((seg.nb, past, dh), lambda kv, qb: (0, 0, kv))
        in_specs = [*q_specs, band(-1), band(0), band(1), cache_spec, band(-1), band(0), band(1), cache_spec,
                    sink_spec]
        args = (*q_args, k, k, k, ck, v, v, v, cv, sink_col)
        n_band = 3
    o_parts = pl.pallas_call(
        functools.partial(_attn_kernel, seg.nb, n_band, nblk, cache_k is not None),
        grid=(nkv, nblk),
        in_specs=in_specs,
        out_specs=[pl.BlockSpec((None, rows, dh), lambda kv, qb: (kv, qb, 0))] * ATTN_GROUP,
        out_shape=[jax.ShapeDtypeStruct((nkv, seg.m, dh), F32)] * ATTN_GROUP,
        compiler_params=_cparams(2),
        name="attn_core",
    )(*args)

    tm_o = _tile_rows(seg, TM_MM)
    o_spec = pl.BlockSpec((nkv, tm_o, dh), lambda i, j: (0, i, 0))
    x_new = _mm(seg, o_parts[0], p['w_o'], _w2_spec(d, 512), d, tn=512, tm=TM_MM, out_dtype=F32, name="attn_wo",
                k=nq * dh, x_spec=o_spec, pro=_pro_heads, pro_args=tuple(('raw', a, o_spec) for a in o_parts[1:]),
                epi=_epi_resid, epi_args=(('tile', x), ('mod', mod, l, 2)))
    return x_new, k, v


def _ffn_up_kernel(nb, n_i, x_ref, xp_ref, xn_ref, wg_ref, wv_ref, cwg_ref, cwv_ref, cbg_ref, cbv_ref, o_ref):
    tm = x_ref.shape[0]
    halo = xp_ref.shape[0]
    xh = _with_halo(x_ref, xp_ref, xn_ref, n_i)

    def conv(w_ref, cw_ref, cb_ref):
        u = jnp.dot(xh, w_ref[...].astype(BF16), preferred_element_type=F32)
        prev = u[halo - nb:halo - nb + tm]
        cur = u[halo:halo + tm]
        nxt = u[halo + nb:halo + nb + tm]
        return prev * cw_ref[0:1, :] + cur * cw_ref[1:2, :] + nxt * cw_ref[2:3, :] + cb_ref[...]

    gate = conv(wg_ref, cwg_ref, cbg_ref)
    val = conv(wv_ref, cwv_ref, cbv_ref)
    o_ref[...] = (gate * jax.nn.sigmoid(gate) * val).astype(o_ref.dtype)


def _conv_ffn(seg, h, x, up, conv_w, conv_b, down, mod, l):
    d = h.shape[1]
    f = down.shape[1]
    tn = 512
    nj = f // tn
    tm = _tile_rows(seg, TM_FFN)
    n_i = seg.m // tm
    halo = max(V7X_BF16_ROWS, seg.nb)
    hp, hn = _halo_specs(tm, n_i, halo, d, lambda j: 0)
    cw = lambda off: pl.BlockSpec((None, conv_w.shape[1], tn), lambda i, j: (l, 0, j + off))
    cb = lambda off: pl.BlockSpec((None, 1, tn), lambda i, j: (l, 0, j + off))
    cb3 = conv_b.reshape(conv_b.shape[0], 1, 2 * f)
    act = pl.pallas_call(
        functools.partial(_ffn_up_kernel, seg.nb, n_i),
        grid=(n_i, nj),
        in_specs=[_full_spec(tm, d), hp, hn, _w3_spec(l, d, tn), _w3_spec(l, d, tn, nj), cw(0), cw(nj), cb(0), cb(nj)],
        out_specs=_tile_spec(tm, tn),
        out_shape=jax.ShapeDtypeStruct((seg.m, f), BF16),
        compiler_params=_cparams(2),
        name="ffn_up",
    )(h, h, h, up, up, conv_w, conv_w, cb3, cb3)
    return _mm(seg, act, down, _w3_spec(l, f, 256), d, tn=256, out_dtype=F32, name="ffn_down",
               epi=_epi_resid, epi_args=(('tile', x), ('mod', mod, l, 5)))


def kernel(x_prompt, x_sample, state_rwkv, cache_k, cache_v, c, c_ctx, ada_w, ada_b, norm_mix, norm_ffn, ffn_up, ffn_conv_w, ffn_conv_b, ffn_down, pool_w, pool_scale, rwkv_mu, rwkv_w_r, rwkv_w_k, rwkv_w_v, rwkv_w0, rwkv_w1, rwkv_w2, rwkv_a0, rwkv_a1, rwkv_a2, rwkv_g1, rwkv_g2, rwkv_k_k, rwkv_k_a, rwkv_r_k, rwkv_ln_w, rwkv_ln_b, rwkv_w_o, attn_w_qkv, attn_q_norm, attn_k_norm, attn_sink, attn_w_o):
    n_ctx, l_ctx, d = x_prompt.shape
    n_lat, l_lat, _ = x_sample.shape
    depth = ada_w.shape[0]
    assert n_ctx % V7X_SUBLANES == 0 and n_lat == V7X_SUBLANES, "8 latent sequences share one conditioning tile"
    assert d // RWKV_HEAD // 2 * V7X_SUBLANES == V7X_LANES, "8 sequences x head pairs must fill the lanes"

    n_cond = n_lat + 1
    ncp = -(-n_cond // V7X_SUBLANES) * V7X_SUBLANES
    cond = jnp.concatenate([c, c_ctx[None, :], jnp.zeros((ncp - n_cond, d), F32)], axis=0)
    mod = _adaln(cond, ada_w, ada_b)
    mod4 = mod.reshape(depth, ncp, 6, d).transpose(0, 2, 1, 3)
    mods = (mod4[:, :, n_lat:n_lat + 1], mod4[:, :, :n_lat])

    segs = (_Seg(n_ctx, l_ctx, 1), _Seg(n_lat, l_lat, n_lat))
    xs = [a.transpose(1, 0, 2).reshape(a.shape[0] * a.shape[1], d) for a in (x_prompt, x_sample)]

    n_mixers = 3
    slots = [0] * n_mixers
    new_state, new_k, new_v = [], [], []
    for l in range(depth):
        kind = l % n_mixers
        slot = slots[kind]
        slots[kind] += 1
        if kind == 1:
            p = {'mu': rwkv_mu[slot], 'w_r': rwkv_w_r[slot], 'w_k': rwkv_w_k[slot], 'w_v': rwkv_w_v[slot],
                 'w0': rwkv_w0[slot], 'w1p': _pad_lora_in(rwkv_w1[slot]), 'w2p': _pad_lora_out(rwkv_w2[slot]),
                 'a0': rwkv_a0[slot], 'a1p': _pad_lora_in(rwkv_a1[slot]), 'a2p': _pad_lora_out(rwkv_a2[slot]),
                 'g1': rwkv_g1[slot], 'g2': rwkv_g2[slot], 'w_o': rwkv_w_o[slot],
                 'kk_p': _param_lanes(rwkv_k_k[slot]), 'ka_p': _param_lanes(rwkv_k_a[slot]),
                 'rk_p': _param_lanes(rwkv_r_k[slot]), 'lnw_p': _param_lanes(rwkv_ln_w[slot]),
                 'lnb_p': _param_lanes(rwkv_ln_b[slot])}
        elif kind == 2:
            p = {'w_qkv': attn_w_qkv[slot], 'q_norm': attn_q_norm[slot], 'k_norm': attn_k_norm[slot],
                 'sink': attn_sink[slot], 'w_o': attn_w_o[slot]}
        for si, seg in enumerate(segs):
            mod_s = mods[si]
            h_dtype = BF16 if kind == 2 else F32
            x = xs[si]
            h = _norm_mod(seg, x, norm_mix, mod_s, l, 0, 1, h_dtype)
            if kind == 0:
                x = _pool_mixer(seg, h, x, pool_w[slot], pool_scale[slot], mod_s, l)
            elif kind == 1:
                x, sfin = _rwkv_mixer(seg, h, x, p, state_rwkv[:, slot] if si == 1 else None, mod_s, l)
                if si == 0:
                    g = seg.nb // V7X_SUBLANES
                    hp = d // RWKV_HEAD // 2
                    st = sfin.reshape(2, g, 2, RWKV_HEAD, RWKV_HEAD, hp, V7X_SUBLANES).transpose(1, 6, 0, 5, 2, 4, 3)
                    new_state.append(st.reshape(seg.nb, 2, 2 * hp, RWKV_HEAD, RWKV_HEAD))
            else:
                x, k, v = _attn_mixer(seg, h, x, p, cache_k[:, slot] if si == 1 else None,
                                      cache_v[:, slot] if si == 1 else None, mod_s, l)
                if si == 0:
                    nkv = k.shape[1] // ATTN_HEAD_DIM
                    to_bm = lambda a: a.reshape(seg.t, seg.nb, nkv, ATTN_HEAD_DIM).transpose(1, 0, 2, 3)
                    new_k.append(to_bm(k))
                    new_v.append(to_bm(v))
            h2 = _norm_mod(seg, x, norm_ffn, mod_s, l, 3, 4, BF16)
            xs[si] = _conv_ffn(seg, h2, x, ffn_up, ffn_conv_w, ffn_conv_b, ffn_down, mod_s, l)

    ys = [a.reshape(seg.t, seg.nb, d).transpose(1, 0, 2) for a, seg in zip(xs, segs)]
    return (ys[0], ys[1], jnp.stack(new_state, axis=1), jnp.stack(new_k, axis=1), jnp.stack(new_v, axis=1))
```
